```python
import math
import jax, jax.numpy as jnp
from jax import lax
import numpy as np

D_MODEL = 1024
BATCH = 4
SEQ = 4096
DEPTH = 4

EPS = 1e-6
N_EVEN = (DEPTH + 1) // 2
N_ODD = DEPTH // 2
A_WIDTH = D_MODEL // 2
SGU_HEADS = 4
SGU_HEAD_DIM = A_WIDTH // SGU_HEADS
SGU_CHUNK = 128
B_WIDTH = D_MODEL - A_WIDTH
HGRN_HEADS = 4
HGRN_HEAD_DIM = B_WIDTH // HGRN_HEADS
HGRN_CHUNK = 64
C_WIDTH = D_MODEL // 2
CONV_WIDTH = 3
D_WIDTH = D_MODEL - C_WIDTH
DIFF_HEADS = 4
DIFF_V_DIM = D_WIDTH // DIFF_HEADS
DIFF_QK_DIM = DIFF_V_DIM // 2
ATTN_BLOCK = 128
ROPE_THETA = 10000.0
D_FF = 4 * D_MODEL

EVEN_SIZES = (A_WIDTH, A_WIDTH, B_WIDTH, B_WIDTH, B_WIDTH, B_WIDTH, B_WIDTH)
ODD_SIZES = (C_WIDTH, C_WIDTH, C_WIDTH, D_WIDTH, D_WIDTH, D_WIDTH)
EVEN_IN = sum(EVEN_SIZES)
ODD_IN = sum(ODD_SIZES)

kernel_name = "hybrid_sgu_hgrn2_shortconv_diffattn_encoder"


def split_cols(t, sizes):
    out, start = [], 0
    for s in sizes:
        out.append(t[..., start:start + s])
        start += s
    return out


def rmsnorm(x, g):
    xf = x.astype(jnp.float32)
    y = xf * lax.rsqrt(jnp.mean(xf * xf, axis=-1, keepdims=True) + EPS)
    return (y * g.astype(jnp.float32)).astype(x.dtype)


def rope_tables(seq, dim):
    inv = 1.0 / (ROPE_THETA ** (jnp.arange(0, dim, 2, dtype=jnp.float32) / dim))
    ang = jnp.arange(seq, dtype=jnp.float32)[:, None] * inv[None, :]
    return jnp.cos(ang), jnp.sin(ang)


def apply_rope(x, cos, sin):
    c = cos[None, :, None, None, :]
    s = sin[None, :, None, None, :]
    x1, x2 = jnp.split(x.astype(jnp.float32), 2, axis=-1)
    return jnp.concatenate([x1 * c - x2 * s, x2 * c + x1 * s], axis=-1).astype(x.dtype)


def sgu_mixer(u, v, norm_g, w_s, b_s):
    bsz, s, _ = u.shape
    u = jax.nn.gelu(u)
    v = rmsnorm(jax.nn.gelu(v).reshape(bsz, s, SGU_HEADS, SGU_HEAD_DIM),
                norm_g.reshape(SGU_HEADS, SGU_HEAD_DIM))
    vc = v.reshape(bsz, s // SGU_CHUNK, SGU_CHUNK, SGU_HEADS, SGU_HEAD_DIM)
    mixed = jnp.einsum('hpq,bnqhc->bnphc', w_s.astype(vc.dtype), vc) \
        + b_s.T.astype(vc.dtype)[None, None, :, :, None]
    return u * mixed.reshape(bsz, s, A_WIDTH)


def hgrn2_bidir(q, i, g, f_fwd, f_bwd, lb, norm_g):
    bsz, s, _ = q.shape
    h, dh = HGRN_HEADS, HGRN_HEAD_DIM
    lbf = lb.astype(jnp.float32)[:, None, None, :]
    f = lbf + (1.0 - lbf) * jax.nn.sigmoid(jnp.stack([f_fwd, f_bwd]).astype(jnp.float32))
    logf = jnp.log(f)
    k = 1.0 - f
    qf = q.astype(jnp.float32)
    vf = i.astype(jnp.float32)
    qd = jnp.stack([qf, qf[:, ::-1]])
    vd = jnp.stack([vf, vf[:, ::-1]])
    kd = jnp.stack([k[0], k[1][:, ::-1]])
    gd = jnp.stack([logf[0], logf[1][:, ::-1]])
    n_chunks = s // HGRN_CHUNK

    def to_chunks(t):
        return jnp.moveaxis(t.reshape(2, bsz, n_chunks, HGRN_CHUNK, h, dh), 2, 0)

    tril = jnp.tril(jnp.ones((HGRN_CHUNK, HGRN_CHUNK), dtype=bool))[None, None, :, :, None, None]

    def step(state, inp):
        qc, kc, vc, gc = inp
        b = jnp.cumsum(gc, axis=2)
        o_inter = jnp.einsum('zbthk,zbhkv->zbthv', qc * jnp.exp(b), state)
        diff = b[:, :, :, None] - b[:, :, None, :]
        decay = jnp.exp(jnp.where(tril, diff, -jnp.inf))
        att = jnp.einsum('zbthk,zbtjhk,zbjhk->zbhtj', qc, decay, kc)
        o_intra = jnp.einsum('zbhtj,zbjhv->zbthv', att, vc)
        b_last = b[:, :, -1]
        k_dec = kc * jnp.exp(b_last[:, :, None] - b)
        state = jnp.exp(b_last)[..., None] * state + jnp.einsum('zbjhk,zbjhv->zbhkv', k_dec, vc)
        return state, o_inter + o_intra

    s0 = jnp.zeros((2, bsz, h, dh, dh), jnp.float32)
    _, o = lax.scan(step, s0, (to_chunks(qd), to_chunks(kd), to_chunks(vd), to_chunks(gd)))
    o = jnp.moveaxis(o, 0, 2).reshape(2, bsz, s, h, dh)
    o = o[0] + o[1][:, ::-1]
    o = rmsnorm(o, norm_g.reshape(h, dh)) * jax.nn.silu(g.astype(jnp.float32).reshape(bsz, s, h, dh))
    return o.reshape(bsz, s, B_WIDTH).astype(q.dtype)


def short_conv_mixer(h_in, b_gate, c_gate, conv_w):
    z = c_gate * h_in
    pad = CONV_WIDTH // 2
    y = lax.conv_general_dilated(z, conv_w[:, None, :].astype(z.dtype), window_strides=(1,),
                                 padding=[(pad, pad)], dimension_numbers=('NWC', 'WIO', 'NWC'),
                                 feature_group_count=C_WIDTH)
    return b_gate * y


def diff_attention(q, k, v, q_g, k_g, lq1, lk1, lq2, lk2, sub_g, lambda_init, cos, sin):
    bsz, s, _ = q.shape
    h, dq, dv = DIFF_HEADS, DIFF_QK_DIM, DIFF_V_DIM
    qh = apply_rope(rmsnorm(q.reshape(bsz, s, h, 2, dq), q_g), cos, sin)
    kh = apply_rope(rmsnorm(k.reshape(bsz, s, h, 2, dq), k_g), cos, sin)
    kf = kh.astype(jnp.float32)
    vf = v.reshape(bsz, s, h, dv).astype(jnp.float32)
    lam = (jnp.exp(jnp.sum(lq1.astype(jnp.float32) * lk1.astype(jnp.float32)))
           - jnp.exp(jnp.sum(lq2.astype(jnp.float32) * lk2.astype(jnp.float32))) + lambda_init)
    scale = dq ** -0.5
    qb = jnp.moveaxis(qh.reshape(bsz, s // ATTN_BLOCK, ATTN_BLOCK, h, 2, dq), 1, 0)

    def block(qblk):
        sc = jnp.einsum('bqhcd,bkhcd->bhcqk', qblk.astype(jnp.float32), kf) * scale
        p = jax.nn.softmax(sc, axis=-1)
        a = p[:, :, 0] - lam * p[:, :, 1]
        return jnp.einsum('bhqk,bkhv->bqhv', a, vf)

    o = lax.map(block, qb)
    o = jnp.moveaxis(o, 0, 1).reshape(bsz, s, h, dv)
    o = rmsnorm(o, sub_g) * (1.0 - lambda_init)
    return o.reshape(bsz, s, D_WIDTH).astype(q.dtype)


def setup_inputs(seed: int = 0) -> dict:
    key = jax.random.key(seed)
    ks = jax.random.split(key, 24)

    def nrm(k, shape, scale):
        return jax.random.normal(k, shape, jnp.float32) * scale

    def gain(k, shape):
        return 1.0 + 0.02 * jax.random.normal(k, shape, jnp.float32)

    return {
        "x": nrm(ks[0], (BATCH, SEQ, D_MODEL), 1.0),
        "norm_mix_g": gain(ks[1], (DEPTH, D_MODEL)),
        "norm_mlp_g": gain(ks[2], (DEPTH, D_MODEL)),
        "w_in_even": nrm(ks[3], (N_EVEN, D_MODEL, EVEN_IN), D_MODEL ** -0.5),
        "w_out_even": nrm(ks[4], (N_EVEN, A_WIDTH + B_WIDTH, D_MODEL), (A_WIDTH + B_WIDTH) ** -0.5),
        "sgu_norm_g": gain(ks[5], (N_EVEN, A_WIDTH)),
        "sgu_w": nrm(ks[6], (N_EVEN, SGU_HEADS, SGU_CHUNK, SGU_CHUNK), SGU_CHUNK ** -0.5),
        "sgu_b": 1.0 + 0.01 * jax.random.normal(ks[7], (N_EVEN, SGU_HEADS, SGU_CHUNK), jnp.float32),
        "hgrn_lb_logits": nrm(ks[8], (2, N_EVEN, B_WIDTH), 0.1),
        "hgrn_norm_g": gain(ks[9], (N_EVEN, B_WIDTH)),
        "w_in_odd": nrm(ks[10], (N_ODD, D_MODEL, ODD_IN), D_MODEL ** -0.5),
        "w_out_odd": nrm(ks[11], (N_ODD, C_WIDTH + D_WIDTH, D_MODEL), (C_WIDTH + D_WIDTH) ** -0.5),
        "conv_w": nrm(ks[12], (N_ODD, CONV_WIDTH, C_WIDTH), CONV_WIDTH ** -0.5),
        "q_norm_g": gain(ks[13], (N_ODD, DIFF_QK_DIM)),
        "k_norm_g": gain(ks[14], (N_ODD, DIFF_QK_DIM)),
        "lambda_q1": nrm(ks[15], (N_ODD, DIFF_QK_DIM), 0.1),
        "lambda_k1": nrm(ks[16], (N_ODD, DIFF_QK_DIM), 0.1),
        "lambda_q2": nrm(ks[17], (N_ODD, DIFF_QK_DIM), 0.1),
        "lambda_k2": nrm(ks[18], (N_ODD, DIFF_QK_DIM), 0.1),
        "diff_norm_g": gain(ks[19], (N_ODD, DIFF_V_DIM)),
        "mlp_w1": nrm(ks[20], (DEPTH, D_MODEL, D_FF), D_MODEL ** -0.5),
        "mlp_w2": nrm(ks[21], (DEPTH, D_FF, D_MODEL), D_FF ** -0.5),
    }


def reference(x, norm_mix_g, norm_mlp_g, w_in_even, w_out_even, sgu_norm_g, sgu_w, sgu_b,
              hgrn_lb_logits, hgrn_norm_g, w_in_odd, w_out_odd, conv_w, q_norm_g, k_norm_g,
              lambda_q1, lambda_k1, lambda_q2, lambda_k2, diff_norm_g, mlp_w1, mlp_w2):
    s = x.shape[1]
    cos, sin = rope_tables(s, DIFF_QK_DIM)
    p_lb = jax.nn.softmax(hgrn_lb_logits.astype(jnp.float32), axis=1)
    lower_bounds = jnp.cumsum(p_lb, axis=1) - p_lb[:, :1]
    for l in range(DEPTH):
        h = rmsnorm(x, norm_mix_g[l])
        if l % 2 == 0:
            e = l // 2
            u, v, q, i, g, f_fwd, f_bwd = split_cols(h @ w_in_even[e], EVEN_SIZES)
            out_a = sgu_mixer(u, v, sgu_norm_g[e], sgu_w[e], sgu_b[e])
            out_b = hgrn2_bidir(q, i, g, f_fwd, f_bwd, lower_bounds[:, e], hgrn_norm_g[e])
            mix = jnp.concatenate([out_a, out_b], axis=-1) @ w_out_even[e]
        else:
            o = l // 2
            h_in, b_gate, c_gate, q, k, v = split_cols(h @ w_in_odd[o], ODD_SIZES)
            out_c = short_conv_mixer(h_in, b_gate, c_gate, conv_w[o])
            lambda_init = 0.8 - 0.6 * math.exp(-0.3 * l)
            out_d = diff_attention(q, k, v, q_norm_g[o], k_norm_g[o], lambda_q1[o], lambda_k1[o],
                                   lambda_q2[o], lambda_k2[o], diff_norm_g[o], lambda_init, cos, sin)
            mix = jnp.concatenate([out_c, out_d], axis=-1) @ w_out_odd[o]
        x = x + mix
        hm = rmsnorm(x, norm_mlp_g[l]) @ mlp_w1[l]
        x = x + jnp.square(jax.nn.relu(hm)) @ mlp_w2[l]
    return x
```

```python
import functools
import math

import numpy as np
import jax
import jax.numpy as jnp
from jax import lax
from jax.experimental import pallas as pl
from jax.experimental.pallas import tpu as pltpu

F32 = jnp.float32
BF16 = jnp.bfloat16

EPS = 1e-6
ROPE_THETA = 10000.0
LOG2E = 1.4426950408889634

HEAD_W = 128
N_HEADS = 4
MIX_W = HEAD_W * N_HEADS
SGU_CHUNK = 128
HGRN_CHUNK = 128
QK_DIM = 64

VMEM_LIMIT = 56 * 1024 * 1024


def _cparams(sem):
    return pltpu.CompilerParams(dimension_semantics=sem, vmem_limit_bytes=VMEM_LIMIT)


def _dot(a, b):
    return jnp.dot(a, b, preferred_element_type=F32)


def _dot_nt(a, b):
    return lax.dot_general(a, b, (((1,), (1,)), ((), ())), preferred_element_type=F32)


def _split2(x):
    hi = x.astype(BF16)
    lo = (x - hi.astype(F32)).astype(BF16)
    return hi, lo


def _rms_rows(x, g):
    ms = jnp.mean(x * x, axis=-1, keepdims=True)
    return x * lax.rsqrt(ms + EPS) * g


def _norm_proj_kernel(x_ref, g_ref, w_ref, *o_refs):
    h = _rms_rows(x_ref[...], g_ref[...]).astype(BF16)
    for s, o_ref in enumerate(o_refs):
        o_ref[...] = _dot(h, w_ref[:, s * MIX_W:(s + 1) * MIX_W])


def norm_proj(x2, g, w_bf, tm=512):
    t, d = x2.shape
    n = w_bf.shape[1]
    n_split = n // MIX_W
    return pl.pallas_call(
        _norm_proj_kernel,
        grid=(t // tm,),
        in_specs=[
            pl.BlockSpec((tm, d), lambda i: (i, 0)),
            pl.BlockSpec((1, d), lambda i: (0, 0)),
            pl.BlockSpec((d, n), lambda i: (0, 0)),
        ],
        out_specs=[pl.BlockSpec((tm, MIX_W), lambda i: (i, 0)) for _ in range(n_split)],
        out_shape=[jax.ShapeDtypeStruct((t, MIX_W), F32) for _ in range(n_split)],
        compiler_params=_cparams(("parallel",)),
        name="norm_proj",
    )(x2, g.reshape(1, d), w_bf)


def _out_proj_kernel(x_ref, a_ref, b_ref, w_ref, o_ref):
    acc = _dot(a_ref[...].astype(BF16), w_ref[:MIX_W, :])
    acc += _dot(b_ref[...].astype(BF16), w_ref[MIX_W:, :])
    o_ref[...] = x_ref[...] + acc


def out_proj(x2, a, b, w_bf, tm=512):
    t, d = x2.shape
    return pl.pallas_call(
        _out_proj_kernel,
        grid=(t // tm,),
        in_specs=[
            pl.BlockSpec((tm, d), lambda i: (i, 0)),
            pl.BlockSpec((tm, MIX_W), lambda i: (i, 0)),
            pl.BlockSpec((tm, MIX_W), lambda i: (i, 0)),
            pl.BlockSpec((2 * MIX_W, d), lambda i: (0, 0)),
        ],
        out_specs=pl.BlockSpec((tm, d), lambda i: (i, 0)),
        out_shape=jax.ShapeDtypeStruct((t, d), F32),
        compiler_params=_cparams(("parallel",)),
        name="out_proj",
    )(x2, a, b, w_bf)


def _mlp_kernel(x_ref, g_ref, w1_ref, w2_ref, o_ref, h_ref, acc_ref):
    j = pl.program_id(1)

    @pl.when(j == 0)
    def _():
        h_ref[...] = _rms_rows(x_ref[...], g_ref[...]).astype(BF16)
        acc_ref[...] = x_ref[...]

    hm = _dot(h_ref[...], w1_ref[...])
    act = jnp.square(jnp.maximum(hm, 0.0)).astype(BF16)
    acc_ref[...] += _dot(act, w2_ref[...])

    @pl.when(j == pl.num_programs(1) - 1)
    def _():
        o_ref[...] = acc_ref[...]


def mlp(x2, g, w1_bf, w2_bf, tm=1024, tf=512):
    t, d = x2.shape
    ff = w1_bf.shape[1]
    return pl.pallas_call(
        _mlp_kernel,
        grid=(t // tm, ff // tf),
        in_specs=[
            pl.BlockSpec((tm, d), lambda i, j: (i, 0)),
            pl.BlockSpec((1, d), lambda i, j: (0, 0)),
            pl.BlockSpec((d, tf), lambda i, j: (0, j)),
            pl.BlockSpec((tf, d), lambda i, j: (j, 0)),
        ],
        out_specs=pl.BlockSpec((tm, d), lambda i, j: (i, 0)),
        out_shape=jax.ShapeDtypeStruct((t, d), F32),
        scratch_shapes=[pltpu.VMEM((tm, d), BF16), pltpu.VMEM((tm, d), F32)],
        compiler_params=_cparams(("parallel", "arbitrary")),
        name="mlp",
    )(x2, g.reshape(1, d), w1_bf, w2_bf)


def _sgu_kernel(u_ref, v_ref, g_ref, w_ref, b_ref, o_ref, *, n_chunks):
    for c in range(n_chunks):
        rows = slice(c * SGU_CHUNK, (c + 1) * SGU_CHUNK)
        for h in range(N_HEADS):
            cols = slice(h * HEAD_W, (h + 1) * HEAD_W)
            vn = _rms_rows(jax.nn.gelu(v_ref[rows, cols]), g_ref[:, cols])
            mixed = _dot(w_ref[h], vn.astype(BF16)) + b_ref[h]
            o_ref[rows, cols] = jax.nn.gelu(u_ref[rows, cols]) * mixed


def sgu(u, v, norm_g, w_s_bf, b_full, ts=512):
    t = u.shape[0]
    return pl.pallas_call(
        functools.partial(_sgu_kernel, n_chunks=ts // SGU_CHUNK),
        grid=(t // ts,),
        in_specs=[
            pl.BlockSpec((ts, MIX_W), lambda i: (i, 0)),
            pl.BlockSpec((ts, MIX_W), lambda i: (i, 0)),
            pl.BlockSpec((1, MIX_W), lambda i: (0, 0)),
            pl.BlockSpec((N_HEADS, SGU_CHUNK, SGU_CHUNK), lambda i: (0, 0, 0)),
            pl.BlockSpec((N_HEADS, SGU_CHUNK, HEAD_W), lambda i: (0, 0, 0)),
        ],
        out_specs=pl.BlockSpec((ts, MIX_W), lambda i: (i, 0)),
        out_shape=jax.ShapeDtypeStruct((t, MIX_W), F32),
        compiler_params=_cparams(("parallel",)),
        name="sgu",
    )(u, v, norm_g.reshape(1, MIX_W), w_s_bf, b_full)


def _hgrn_tables(c):
    levels = int(math.log2(c))
    assert 1 << levels == c
    pos = np.arange(c)
    t = pos[:, None]
    i = pos[None, :]
    d_rows, m_rows = [], []
    for lv in range(levels):
        h = c >> (lv + 1)
        start = (t // (2 * h)) * (2 * h)
        mid = start + h - 1
        upper = (t - start) >= h
        d_up = (i > mid) & (i <= t)
        d_lo = (i > t) & (i <= mid)
        d_rows.append(np.where(upper, d_up, d_lo))
        same = (t // (2 * h)) == (i // (2 * h))
        m_rows.append(same & upper & ((i % (2 * h)) < h))
    d_rows.append(i <= t)
    d_rows.append(i > t)
    d = np.concatenate(d_rows, axis=0).astype(np.float32)
    m = np.concatenate(m_rows, axis=0).astype(np.float32)
    return d, m, levels


def _flip_blocks(a, c):
    blocks = a.reshape(-1, c, c)
    return blocks[:, ::-1, ::-1].reshape(a.shape)


def _hgrn_direction(q, v_bf, vt_bf, a, lb, d_ref, m_ref, st_ref, *, levels, last_row):
    c = q.shape[0]
    f = lb + (1.0 - lb) * (1.0 / (1.0 + jnp.exp(-a)))
    lf = jnp.log(f) * LOG2E
    kk = 1.0 - f
    hi, lo = _split2(lf)
    e_all = jnp.exp2(_dot(d_ref[...], hi) + _dot(d_ref[...], lo))
    row = lax.broadcasted_iota(jnp.int32, (c, c), 0)
    col = lax.broadcasted_iota(jnp.int32, (c, c), 1)
    eye = row == col
    outs = []
    for h in range(N_HEADS):
        cols = slice(h * HEAD_W, (h + 1) * HEAD_W)
        qh, kh = q[:, cols], kk[:, cols]
        att = jnp.where(eye, jnp.sum(qh * kh, axis=-1, keepdims=True), 0.0)
        for lv in range(levels):
            e_lv = e_all[lv * c:(lv + 1) * c, cols]
            att += _dot_nt((qh * e_lv).astype(BF16), (kh * e_lv).astype(BF16)) * m_ref[lv * c:(lv + 1) * c, :]
        e_inc = e_all[levels * c:(levels + 1) * c, cols]
        e_exc = e_all[(levels + 1) * c:(levels + 2) * c, cols]
        st = st_ref[h]
        o_h = _dot(att.astype(BF16), v_bf[:, cols])
        o_h += _dot_nt((qh * e_inc).astype(BF16), st.astype(BF16))
        k_dec = (kh * e_exc).astype(BF16)
        st_ref[h] = st * e_inc[last_row:last_row + 1, :] + _dot(vt_bf[cols, :], k_dec)
        outs.append(o_h)
    return jnp.concatenate(outs, axis=-1)


def _hgrn_finalize(o, g, ng):
    outs = []
    for h in range(N_HEADS):
        cols = slice(h * HEAD_W, (h + 1) * HEAD_W)
        gh = g[:, cols]
        outs.append(_rms_rows(o[:, cols], ng[:, cols]) * (gh * (1.0 / (1.0 + jnp.exp(-gh)))))
    return jnp.concatenate(outs, axis=-1)


def _hgrn_kernel(qf_ref, qb_ref, if_ref, ib_ref, gf_ref, gb_ref, af_ref, ab_ref,
                 lbf_ref, lbb_ref, ng_ref, df_ref, db_ref, mf_ref, mb_ref,
                 o_ref, stf_ref, stb_ref, *, levels, chunk):
    n = pl.program_id(1)
    n_chunks = pl.num_programs(1)

    @pl.when(n == 0)
    def _():
        stf_ref[...] = jnp.zeros_like(stf_ref)
        stb_ref[...] = jnp.zeros_like(stb_ref)

    def run(q_ref, i_ref, a_ref, lb_ref, d_ref, m_ref, st_ref, last_row):
        v = i_ref[...]
        return _hgrn_direction(q_ref[...], v.astype(BF16), v.T.astype(BF16), a_ref[...], lb_ref[...],
                               d_ref, m_ref, st_ref, levels=levels, last_row=last_row)

    o_f = run(qf_ref, if_ref, af_ref, lbf_ref, df_ref, mf_ref, stf_ref, chunk - 1)
    o_b = run(qb_ref, ib_ref, ab_ref, lbb_ref, db_ref, mb_ref, stb_ref, 0)
    rows_f = pl.ds(pl.multiple_of(n * chunk, chunk), chunk)
    rows_b = pl.ds(pl.multiple_of((n_chunks - 1 - n) * chunk, chunk), chunk)

    @pl.when(2 * n < n_chunks)
    def _():
        o_ref[rows_f, :] = o_f
        o_ref[rows_b, :] = o_b

    @pl.when(2 * n >= n_chunks)
    def _():
        o_ref[rows_f, :] = _hgrn_finalize(o_ref[rows_f, :] + o_f, gf_ref[...], ng_ref[...])
        o_ref[rows_b, :] = _hgrn_finalize(o_ref[rows_b, :] + o_b, gb_ref[...], ng_ref[...])


def hgrn2(q, i, g, a_fwd, a_bwd, lb, norm_g, bsz, seq, chunk=HGRN_CHUNK):
    n_chunks = seq // chunk
    assert n_chunks % 2 == 0
    d_np, m_np, levels = _hgrn_tables(chunk)
    d_f = jnp.asarray(d_np, BF16)
    d_b = jnp.asarray(_flip_blocks(d_np, chunk), BF16)
    m_f = jnp.asarray(m_np, F32)
    m_b = jnp.asarray(_flip_blocks(m_np, chunk), F32)

    fwd = pl.BlockSpec((chunk, MIX_W), lambda b, n: (b * n_chunks + n, 0))
    bwd = pl.BlockSpec((chunk, MIX_W), lambda b, n: (b * n_chunks + n_chunks - 1 - n, 0))
    vec = pl.BlockSpec((1, MIX_W), lambda b, n: (0, 0))

    def table(a):
        return pl.BlockSpec(a.shape, lambda b, n: (0, 0))

    return pl.pallas_call(
        functools.partial(_hgrn_kernel, levels=levels, chunk=chunk),
        grid=(bsz, n_chunks),
        in_specs=[fwd, bwd, fwd, bwd, fwd, bwd, fwd, bwd, vec, vec, vec,
                  table(d_f), table(d_b), table(m_f), table(m_b)],
        out_specs=pl.BlockSpec((seq, MIX_W), lambda b, n: (b, 0)),
        out_shape=jax.ShapeDtypeStruct((bsz * seq, MIX_W), F32),
        scratch_shapes=[pltpu.VMEM((N_HEADS, HEAD_W, HEAD_W), F32),
                        pltpu.VMEM((N_HEADS, HEAD_W, HEAD_W), F32)],
        compiler_params=_cparams(("parallel", "arbitrary")),
        name="hgrn2",
    )(q, q, i, i, g, g, a_fwd, a_bwd, lb[0:1], lb[1:2], norm_g.reshape(1, MIX_W),
      d_f, d_b, m_f, m_b)


CONV_ROWS = 512
CONV_PAD = 8


def _conv_kernel(h_ref, b_ref, c_ref, w_ref, o_ref, z_ref, *, seq):
    zeros = jnp.zeros((CONV_PAD, HEAD_W), F32)
    z_ref[0:CONV_PAD, :] = zeros
    z_ref[seq + CONV_PAD:seq + 2 * CONV_PAD, :] = zeros
    n_blocks = seq // CONV_ROWS

    def gate(r, carry):
        rows = pl.ds(pl.multiple_of(r * CONV_ROWS, CONV_ROWS), CONV_ROWS)
        z_ref[pl.ds(pl.multiple_of(r * CONV_ROWS + CONV_PAD, CONV_PAD), CONV_ROWS), :] = c_ref[rows, :] * h_ref[rows, :]
        return carry

    lax.fori_loop(0, n_blocks, gate, 0)
    w_prev, w_mid, w_next = w_ref[0:1, :], w_ref[1:2, :], w_ref[2:3, :]
    ext = CONV_ROWS + 2 * CONV_PAD

    def conv(r, carry):
        zc = z_ref[pl.ds(pl.multiple_of(r * CONV_ROWS, CONV_PAD), ext), :]
        prev = pltpu.roll(zc, 1, 0)[CONV_PAD:CONV_PAD + CONV_ROWS, :]
        nxt = pltpu.roll(zc, ext - 1, 0)[CONV_PAD:CONV_PAD + CONV_ROWS, :]
        mid = zc[CONV_PAD:CONV_PAD + CONV_ROWS, :]
        rows = pl.ds(pl.multiple_of(r * CONV_ROWS, CONV_ROWS), CONV_ROWS)
        o_ref[rows, :] = b_ref[rows, :] * (w_prev * prev + w_mid * mid + w_next * nxt)
        return carry

    lax.fori_loop(0, n_blocks, conv, 0)


def short_conv(h_in, b_gate, c_gate, conv_w, bsz, seq):
    blk = pl.BlockSpec((seq, HEAD_W), lambda b, c: (b, c))
    return pl.pallas_call(
        functools.partial(_conv_kernel, seq=seq),
        grid=(bsz, N_HEADS),
        in_specs=[blk, blk, blk, pl.BlockSpec((3, HEAD_W), lambda b, c: (0, c))],
        out_specs=blk,
        out_shape=jax.ShapeDtypeStruct((bsz * seq, MIX_W), F32),
        scratch_shapes=[pltpu.VMEM((seq + 2 * CONV_PAD, HEAD_W), F32)],
        compiler_params=_cparams(("parallel", "parallel")),
        name="short_conv",
    )(h_in, b_gate, c_gate, conv_w)


ATTN_TQ = 256
ATTN_PREP_ROWS = 512


def _qk_prep(x, gain, cos, sin_signed, seg_mean):
    hi, lo = _split2(x * x)
    ms = _dot(hi, seg_mean) + _dot(lo, seg_mean)
    y = x * lax.rsqrt(ms + EPS) * gain
    lane = lax.broadcasted_iota(jnp.int32, y.shape, 1)
    half = QK_DIM // 2
    partner = jnp.where((lane % QK_DIM) < half,
                        pltpu.roll(y, HEAD_W - half, 1), pltpu.roll(y, half, 1))
    return y * cos + partner * sin_signed


def _attn_kernel(q_ref, k_ref, v_ref, cosq_ref, sinq_ref, cosk_ref, sink_ref, qg_ref, kg_ref,
                 sg_ref, lam_ref, mean_ref, o_ref, kh_ref, v1_ref, *, seq, out_scale):
    qi = pl.program_id(2)
    seg_mean = mean_ref[...]

    @pl.when(qi == 0)
    def _():
        def prep(r, carry):
            rows = pl.ds(pl.multiple_of(r * ATTN_PREP_ROWS, ATTN_PREP_ROWS), ATTN_PREP_ROWS)
            kh_ref[rows, :] = _qk_prep(k_ref[rows, :], kg_ref[...], cosk_ref[rows, :],
                                       sink_ref[rows, :], seg_mean).astype(BF16)
            v1_ref[rows, 0:HEAD_W] = v_ref[rows, :].astype(BF16)
            v1_ref[rows, HEAD_W:2 * HEAD_W] = jnp.ones((ATTN_PREP_ROWS, HEAD_W), BF16)
            return carry

        lax.fori_loop(0, seq // ATTN_PREP_ROWS, prep, 0)

    scale = (QK_DIM ** -0.5) * LOG2E
    qh = _qk_prep(q_ref[...], qg_ref[...], cosq_ref[...], sinq_ref[...], seg_mean) * scale
    lane = lax.broadcasted_iota(jnp.int32, qh.shape, 1)
    kh = kh_ref[...]
    v1 = v1_ref[...]
    parts = []
    for comp in range(2):
        in_comp = (lane < QK_DIM) if comp == 0 else (lane >= QK_DIM)
        s = _dot_nt(jnp.where(in_comp, qh, 0.0).astype(BF16), kh)
        p = jnp.exp2(s - jnp.max(s, axis=-1, keepdims=True)).astype(BF16)
        pv = _dot(p, v1)
        parts.append(pv[:, 0:HEAD_W] / pv[:, HEAD_W:2 * HEAD_W])
    o = parts[0] - lam_ref[...] * parts[1]
    o_ref[...] = _rms_rows(o, sg_ref[...]) * out_scale


def diff_attention(q, k, v, cos_t, sin_t, qg, kg, sub_g, lam, lambda_init, bsz, seq, tq=ATTN_TQ):
    nq = seq // tq
    seg = np.kron(np.eye(2, dtype=np.float32), np.full((QK_DIM, QK_DIM), 1.0 / QK_DIM, np.float32))
    qblk = pl.BlockSpec((tq, HEAD_W), lambda b, h, i: (b * nq + i, h))
    kblk = pl.BlockSpec((seq, HEAD_W), lambda b, h, i: (b, h))
    tabq = pl.BlockSpec((tq, HEAD_W), lambda b, h, i: (i, 0))
    tabk = pl.BlockSpec((seq, HEAD_W), lambda b, h, i: (0, 0))
    vec = pl.BlockSpec((1, HEAD_W), lambda b, h, i: (0, 0))
    return pl.pallas_call(
        functools.partial(_attn_kernel, seq=seq, out_scale=1.0 - lambda_init),
        grid=(bsz, N_HEADS, nq),
        in_specs=[qblk, kblk, kblk, tabq, tabq, tabk, tabk, vec, vec, vec, vec,
                  pl.BlockSpec((HEAD_W, HEAD_W), lambda b, h, i: (0, 0))],
        out_specs=qblk,
        out_shape=jax.ShapeDtypeStruct((bsz * seq, MIX_W), F32),
        scratch_shapes=[pltpu.VMEM((seq, HEAD_W), BF16), pltpu.VMEM((seq, 2 * HEAD_W), BF16)],
        compiler_params=_cparams(("parallel", "parallel", "arbitrary")),
        name="diff_attention",
    )(q, k, v, cos_t, sin_t, cos_t, sin_t,
      jnp.tile(qg, 2).reshape(1, HEAD_W), jnp.tile(kg, 2).reshape(1, HEAD_W),
      sub_g.reshape(1, HEAD_W), jnp.full((1, HEAD_W), lam, F32), jnp.asarray(seg, BF16))


def _rope_tables(seq):
    inv = 1.0 / (ROPE_THETA ** (jnp.arange(0, QK_DIM, 2, dtype=F32) / QK_DIM))
    ang = jnp.arange(seq, dtype=F32)[:, None] * inv[None, :]
    cos, sin = jnp.cos(ang), jnp.sin(ang)
    return jnp.tile(cos, (1, 4)), jnp.tile(jnp.concatenate([-sin, sin], axis=-1), (1, 2))


def kernel(x, norm_mix_g, norm_mlp_g, w_in_even, w_out_even, sgu_norm_g, sgu_w, sgu_b,
           hgrn_lb_logits, hgrn_norm_g, w_in_odd, w_out_odd, conv_w, q_norm_g, k_norm_g,
           lambda_q1, lambda_k1, lambda_q2, lambda_k2, diff_norm_g, mlp_w1, mlp_w2):
    bsz, seq, d = x.shape
    depth = norm_mix_g.shape[0]
    cos_t, sin_t = _rope_tables(seq)
    p_lb = jax.nn.softmax(hgrn_lb_logits.astype(F32), axis=1)
    lower_bounds = jnp.cumsum(p_lb, axis=1) - p_lb[:, :1]

    x2 = x.reshape(bsz * seq, d)
    for l in range(depth):
        if l % 2 == 0:
            e = l // 2
            u, v, q, i, g, a_fwd, a_bwd = norm_proj(x2, norm_mix_g[l], w_in_even[e].astype(BF16))
            b_full = jnp.broadcast_to(sgu_b[e][:, :, None], (N_HEADS, SGU_CHUNK, HEAD_W))
            out_a = sgu(u, v, sgu_norm_g[e], sgu_w[e].astype(BF16), b_full)
            out_b = hgrn2(q, i, g, a_fwd, a_bwd, lower_bounds[:, e], hgrn_norm_g[e], bsz, seq)
            x2 = out_proj(x2, out_a, out_b, w_out_even[e].astype(BF16))
        else:
            o = l // 2
            h_in, b_gate, c_gate, q, k, v = norm_proj(x2, norm_mix_g[l], w_in_odd[o].astype(BF16))
            out_c = short_conv(h_in, b_gate, c_gate, conv_w[o], bsz, seq)
            lambda_init = 0.8 - 0.6 * math.exp(-0.3 * l)
            lam = (jnp.exp(jnp.sum(lambda_q1[o] * lambda_k1[o]))
                   - jnp.exp(jnp.sum(lambda_q2[o] * lambda_k2[o])) + lambda_init)
            out_d = diff_attention(q, k, v, cos_t, sin_t, q_norm_g[o], k_norm_g[o], diff_norm_g[o],
                                   lam, lambda_init, bsz, seq)
            x2 = out_proj(x2, out_c, out_d, w_out_odd[o].astype(BF16))
        x2 = mlp(x2, norm_mlp_g[l], mlp_w1[l].astype(BF16), mlp_w2[l].astype(BF16))
    return x2.reshape(bsz, seq, d)
```

```python
import functools
import math

import numpy as np
import jax
import jax.numpy as jnp
from jax import lax
from jax.experimental import pallas as pl
from jax.experimental.pallas import tpu as pltpu

F32 = jnp.float32
BF16 = jnp.bfloat16

EPS = 1e-6
ROPE_THETA = 10000.0
LOG2E = 1.4426950408889634

HEAD_W = 128
N_HEADS = 4
MIX_W = HEAD_W * N_HEADS
SGU_CHUNK = 128
HGRN_CHUNK = 128
QK_DIM = 64

VMEM_LIMIT = 56 * 1024 * 1024


def _cparams(sem):
    return pltpu.CompilerParams(dimension_semantics=sem, vmem_limit_bytes=VMEM_LIMIT)


def _dot(a, b):
    return jnp.dot(a, b, preferred_element_type=F32)


def _dot_nt(a, b):
    return lax.dot_general(a, b, (((1,), (1,)), ((), ())), preferred_element_type=F32)


def _split2(x):
    hi = x.astype(BF16)
    lo = (x - hi.astype(F32)).astype(BF16)
    return hi, lo


def _rms_rows(x, g):
    ms = jnp.mean(x * x, axis=-1, keepdims=True)
    return x * lax.rsqrt(ms + EPS) * g


def _norm_proj_kernel(x_ref, g_ref, w_ref, *o_refs):
    h = _rms_rows(x_ref[...], g_ref[...]).astype(BF16)
    for s, o_ref in enumerate(o_refs):
        o_ref[...] = _dot(h, w_ref[:, s * MIX_W:(s + 1) * MIX_W])


def norm_proj(x2, g, w_bf, tm=512):
    t, d = x2.shape
    n = w_bf.shape[1]
    n_split = n // MIX_W
    return pl.pallas_call(
        _norm_proj_kernel,
        grid=(t // tm,),
        in_specs=[
            pl.BlockSpec((tm, d), lambda i: (i, 0)),
            pl.BlockSpec((1, d), lambda i: (0, 0)),
            pl.BlockSpec((d, n), lambda i: (0, 0)),
        ],
        out_specs=[pl.BlockSpec((tm, MIX_W), lambda i: (i, 0)) for _ in range(n_split)],
        out_shape=[jax.ShapeDtypeStruct((t, MIX_W), F32) for _ in range(n_split)],
        compiler_params=_cparams(("parallel",)),
        name="norm_proj",
    )(x2, g.reshape(1, d), w_bf)


def _out_proj_kernel(x_ref, a_ref, b_ref, w_ref, o_ref):
    acc = _dot(a_ref[...].astype(BF16), w_ref[:MIX_W, :])
    acc += _dot(b_ref[...].astype(BF16), w_ref[MIX_W:, :])
    o_ref[...] = x_ref[...] + acc


def out_proj(x2, a, b, w_bf, tm=512):
    t, d = x2.shape
    return pl.pallas_call(
        _out_proj_kernel,
        grid=(t // tm,),
        in_specs=[
            pl.BlockSpec((tm, d), lambda i: (i, 0)),
            pl.BlockSpec((tm, MIX_W), lambda i: (i, 0)),
            pl.BlockSpec((tm, MIX_W), lambda i: (i, 0)),
            pl.BlockSpec((2 * MIX_W, d), lambda i: (0, 0)),
        ],
        out_specs=pl.BlockSpec((tm, d), lambda i: (i, 0)),
        out_shape=jax.ShapeDtypeStruct((t, d), F32),
        compiler_params=_cparams(("parallel",)),
        name="out_proj",
    )(x2, a, b, w_bf)


def _mlp_kernel(x_ref, g_ref, w1_ref, w2_ref, o_ref, h_ref, acc_ref):
    j = pl.program_id(1)

    @pl.when(j == 0)
    def _():
        h_ref[...] = _rms_rows(x_ref[...], g_ref[...]).astype(BF16)
        acc_ref[...] = x_ref[...]

    hm = _dot(h_ref[...], w1_ref[...])
    act = jnp.square(jnp.maximum(hm, 0.0)).astype(BF16)
    acc_ref[...] += _dot(act, w2_ref[...])

    @pl.when(j == pl.num_programs(1) - 1)
    def _():
        o_ref[...] = acc_ref[...]


def mlp(x2, g, w1_bf, w2_bf, tm=1024, tf=512):
    t, d = x2.shape
    ff = w1_bf.shape[1]
    return pl.pallas_call(
        _mlp_kernel,
        grid=(t // tm, ff // tf),
        in_specs=[
            pl.BlockSpec((tm, d), lambda i, j: (i, 0)),
            pl.BlockSpec((1, d), lambda i, j: (0, 0)),
            pl.BlockSpec((d, tf), lambda i, j: (0, j)),
            pl.BlockSpec((tf, d), lambda i, j: (j, 0)),
        ],
        out_specs=pl.BlockSpec((tm, d), lambda i, j: (i, 0)),
        out_shape=jax.ShapeDtypeStruct((t, d), F32),
        scratch_shapes=[pltpu.VMEM((tm, d), BF16), pltpu.VMEM((tm, d), F32)],
        compiler_params=_cparams(("parallel", "arbitrary")),
        name="mlp",
    )(x2, g.reshape(1, d), w1_bf, w2_bf)


def _sgu_kernel(u_ref, v_ref, g_ref, w_ref, b_ref, o_ref, *, n_chunks):
    for c in range(n_chunks):
        rows = slice(c * SGU_CHUNK, (c + 1) * SGU_CHUNK)
        for h in range(N_HEADS):
            cols = slice(h * HEAD_W, (h + 1) * HEAD_W)
            vn = _rms_rows(jax.nn.gelu(v_ref[rows, cols]), g_ref[:, cols])
            mixed = _dot(w_ref[h], vn.astype(BF16)) + b_ref[h]
            o_ref[rows, cols] = jax.nn.gelu(u_ref[rows, cols]) * mixed


def sgu(u, v, norm_g, w_s_bf, b_full, ts=512):
    t = u.shape[0]
    return pl.pallas_call(
        functools.partial(_sgu_kernel, n_chunks=ts // SGU_CHUNK),
        grid=(t // ts,),
        in_specs=[
            pl.BlockSpec((ts, MIX_W), lambda i: (i, 0)),
            pl.BlockSpec((ts, MIX_W), lambda i: (i, 0)),
            pl.BlockSpec((1, MIX_W), lambda i: (0, 0)),
            pl.BlockSpec((N_HEADS, SGU_CHUNK, SGU_CHUNK), lambda i: (0, 0, 0)),
            pl.BlockSpec((N_HEADS, SGU_CHUNK, HEAD_W), lambda i: (0, 0, 0)),
        ],
        out_specs=pl.BlockSpec((ts, MIX_W), lambda i: (i, 0)),
        out_shape=jax.ShapeDtypeStruct((t, MIX_W), F32),
        compiler_params=_cparams(("parallel",)),
        name="sgu",
    )(u, v, norm_g.reshape(1, MIX_W), w_s_bf, b_full)


def _hgrn_tables(c):
    levels = int(math.log2(c))
    assert 1 << levels == c
    pos = np.arange(c)
    t = pos[:, None]
    i = pos[None, :]
    d_rows, m_rows = [], []
    for lv in range(levels):
        h = c >> (lv + 1)
        start = (t // (2 * h)) * (2 * h)
        mid = start + h - 1
        upper = (t - start) >= h
        d_up = (i > mid) & (i <= t)
        d_lo = (i > t) & (i <= mid)
        d_rows.append(np.where(upper, d_up, d_lo))
        same = (t // (2 * h)) == (i // (2 * h))
        m_rows.append(same & upper & ((i % (2 * h)) < h))
    d_rows.append(i <= t)
    d_rows.append(i > t)
    d = np.concatenate(d_rows, axis=0).astype(np.float32)
    m = np.concatenate(m_rows, axis=0).astype(np.float32)
    return d, m, levels


def _flip_blocks(a, c):
    blocks = a.reshape(-1, c, c)
    return blocks[:, ::-1, ::-1].reshape(a.shape)


def _hgrn_direction(q, v_bf, vt_bf, a, lb, d_ref, m_ref, st_ref, *, levels, last_row):
    c = q.shape[0]
    f = lb + (1.0 - lb) * (1.0 / (1.0 + jnp.exp(-a)))
    lf = jnp.log(f) * LOG2E
    kk = 1.0 - f
    hi, lo = _split2(lf)
    e_all = jnp.exp2(_dot(d_ref[...], hi) + _dot(d_ref[...], lo))
    row = lax.broadcasted_iota(jnp.int32, (c, c), 0)
    col = lax.broadcasted_iota(jnp.int32, (c, c), 1)
    eye = row == col
    outs = []
    for h in range(N_HEADS):
        cols = slice(h * HEAD_W, (h + 1) * HEAD_W)
        qh, kh = q[:, cols], kk[:, cols]
        att = jnp.where(eye, jnp.sum(qh * kh, axis=-1, keepdims=True), 0.0)
        for lv in range(levels):
            e_lv = e_all[lv * c:(lv + 1) * c, cols]
            att += _dot_nt((qh * e_lv).astype(BF16), (kh * e_lv).astype(BF16)) * m_ref[lv * c:(lv + 1) * c, :]
        e_inc = e_all[levels * c:(levels + 1) * c, cols]
        e_exc = e_all[(levels + 1) * c:(levels + 2) * c, cols]
        st = st_ref[h]
        o_h = _dot(att.astype(BF16), v_bf[:, cols])
        o_h += _dot_nt((qh * e_inc).astype(BF16), st.astype(BF16))
        k_dec = (kh * e_exc).astype(BF16)
        st_ref[h] = st * e_inc[last_row:last_row + 1, :] + _dot(vt_bf[cols, :], k_dec)
        outs.append(o_h)
    return jnp.concatenate(outs, axis=-1)


def _hgrn_finalize(o, g, ng):
    outs = []
    for h in range(N_HEADS):
        cols = slice(h * HEAD_W, (h + 1) * HEAD_W)
        gh = g[:, cols]
        outs.append(_rms_rows(o[:, cols], ng[:, cols]) * (gh * (1.0 / (1.0 + jnp.exp(-gh)))))
    return jnp.concatenate(outs, axis=-1)


def _hgrn_kernel(qf_ref, qb_ref, if_ref, ib_ref, gf_ref, gb_ref, af_ref, ab_ref,
                 lbf_ref, lbb_ref, ng_ref, df_ref, db_ref, mf_ref, mb_ref,
                 o_ref, stf_ref, stb_ref, *, levels, chunk):
    n = pl.program_id(1)
    n_chunks = pl.num_programs(1)

    @pl.when(n == 0)
    def _():
        stf_ref[...] = jnp.zeros_like(stf_ref)
        stb_ref[...] = jnp.zeros_like(stb_ref)

    def run(q_ref, i_ref, a_ref, lb_ref, d_ref, m_ref, st_ref, last_row):
        v = i_ref[...]
        return _hgrn_direction(q_ref[...], v.astype(BF16), v.T.astype(BF16), a_ref[...], lb_ref[...],
                               d_ref, m_ref, st_ref, levels=levels, last_row=last_row)

    o_f = run(qf_ref, if_ref, af_ref, lbf_ref, df_ref, mf_ref, stf_ref, chunk - 1)
    o_b = run(qb_ref, ib_ref, ab_ref, lbb_ref, db_ref, mb_ref, stb_ref, 0)
    rows_f = pl.ds(pl.multiple_of(n * chunk, chunk), chunk)
    rows_b = pl.ds(pl.multiple_of((n_chunks - 1 - n) * chunk, chunk), chunk)

    @pl.when(2 * n < n_chunks)
    def _():
        o_ref[rows_f, :] = o_f
        o_ref[rows_b, :] = o_b

    @pl.when(2 * n >= n_chunks)
    def _():
        o_ref[rows_f, :] = _hgrn_finalize(o_ref[rows_f, :] + o_f, gf_ref[...], ng_ref[...])
        o_ref[rows_b, :] = _hgrn_finalize(o_ref[rows_b, :] + o_b, gb_ref[...], ng_ref[...])


def hgrn2(q, i, g, a_fwd, a_bwd, lb, norm_g, bsz, seq, chunk=HGRN_CHUNK):
    n_chunks = seq // chunk
    assert n_chunks % 2 == 0
    d_np, m_np, levels = _hgrn_tables(chunk)
    d_f = jnp.asarray(d_np, BF16)
    d_b = jnp.asarray(_flip_blocks(d_np, chunk), BF16)
    m_f = jnp.asarray(m_np, F32)
    m_b = jnp.asarray(_flip_blocks(m_np, chunk), F32)

    fwd = pl.BlockSpec((chunk, MIX_W), lambda b, n: (b * n_chunks + n, 0))
    bwd = pl.BlockSpec((chunk, MIX_W), lambda b, n: (b * n_chunks + n_chunks - 1 - n, 0))
    vec = pl.BlockSpec((1, MIX_W), lambda b, n: (0, 0))

    def table(a):
        return pl.BlockSpec(a.shape, lambda b, n: (0, 0))

    return pl.pallas_call(
        functools.partial(_hgrn_kernel, levels=levels, chunk=chunk),
        grid=(bsz, n_chunks),
        in_specs=[fwd, bwd, fwd, bwd, fwd, bwd, fwd, bwd, vec, vec, vec,
                  table(d_f), table(d_b), table(m_f), table(m_b)],
        out_specs=pl.BlockSpec((seq, MIX_W), lambda b, n: (b, 0)),
        out_shape=jax.ShapeDtypeStruct((bsz * seq, MIX_W), F32),
        scratch_shapes=[pltpu.VMEM((N_HEADS, HEAD_W, HEAD_W), F32),
                        pltpu.VMEM((N_HEADS, HEAD_W, HEAD_W), F32)],
        compiler_params=_cparams(("parallel", "arbitrary")),
        name="hgrn2",
    )(q, q, i, i, g, g, a_fwd, a_bwd, lb[0:1], lb[1:2], norm_g.reshape(1, MIX_W),
      d_f, d_b, m_f, m_b)


CONV_ROWS = 512
CONV_PAD = 8


def _conv_kernel(h_ref, b_ref, c_ref, w_ref, o_ref, z_ref, *, seq):
    zeros = jnp.zeros((CONV_PAD, HEAD_W), F32)
    z_ref[0:CONV_PAD, :] = zeros
    z_ref[seq + CONV_PAD:seq + 2 * CONV_PAD, :] = zeros
    n_blocks = seq // CONV_ROWS

    def gate(r, carry):
        rows = pl.ds(pl.multiple_of(r * CONV_ROWS, CONV_ROWS), CONV_ROWS)
        z_ref[pl.ds(pl.multiple_of(r * CONV_ROWS + CONV_PAD, CONV_PAD), CONV_ROWS), :] = c_ref[rows, :] * h_ref[rows, :]
        return carry

    lax.fori_loop(0, n_blocks, gate, 0)
    w_prev, w_mid, w_next = w_ref[0:1, :], w_ref[1:2, :], w_ref[2:3, :]
    ext = CONV_ROWS + 2 * CONV_PAD

    def conv(r, carry):
        zc = z_ref[pl.ds(pl.multiple_of(r * CONV_ROWS, CONV_PAD), ext), :]
        prev = pltpu.roll(zc, 1, 0)[CONV_PAD:CONV_PAD + CONV_ROWS, :]
        nxt = pltpu.roll(zc, ext - 1, 0)[CONV_PAD:CONV_PAD + CONV_ROWS, :]
        mid = zc[CONV_PAD:CONV_PAD + CONV_ROWS, :]
        rows = pl.ds(pl.multiple_of(r * CONV_ROWS, CONV_ROWS), CONV_ROWS)
        o_ref[rows, :] = b_ref[rows, :] * (w_prev * prev + w_mid * mid + w_next * nxt)
        return carry

    lax.fori_loop(0, n_blocks, conv, 0)


def short_conv(h_in, b_gate, c_gate, conv_w, bsz, seq):
    blk = pl.BlockSpec((seq, HEAD_W), lambda b, c: (b, c))
    return pl.pallas_call(
        functools.partial(_conv_kernel, seq=seq),
        grid=(bsz, N_HEADS),
        in_specs=[blk, blk, blk, pl.BlockSpec((3, HEAD_W), lambda b, c: (0, c))],
        out_specs=blk,
        out_shape=jax.ShapeDtypeStruct((bsz * seq, MIX_W), F32),
        scratch_shapes=[pltpu.VMEM((seq + 2 * CONV_PAD, HEAD_W), F32)],
        compiler_params=_cparams(("parallel", "parallel")),
        name="short_conv",
    )(h_in, b_gate, c_gate, conv_w)


ATTN_TQ = 1024
ATTN_TW = 512


def _qk_prep(x, gain, cos, sin_signed, seg_mean):
    hi, lo = _split2(x * x)
    ms = _dot(hi, seg_mean) + _dot(lo, seg_mean)
    y = x * lax.rsqrt(ms + EPS) * gain
    lane = lax.broadcasted_iota(jnp.int32, y.shape, 1)
    half = QK_DIM // 2
    partner = jnp.where((lane % QK_DIM) < half,
                        pltpu.roll(y, HEAD_W - half, 1), pltpu.roll(y, half, 1))
    return y * cos + partner * sin_signed


ATTN_ONES_ROWS = 16
ATTN_KB = 512


def _attn_kernel(q_ref, k_ref, v_ref, cosq_ref, sinq_ref, cosk_ref, sink_ref, qg_ref, kg_ref,
                 sg_ref, lam_ref, mean_ref, o_ref, kh_ref, v1t_ref, qp_ref, s0_ref, s1_ref, pa_ref, pb_ref,
                 acc_ref, *, seq, out_scale):
    qi = pl.program_id(2)
    seg_mean = mean_ref[...]
    n_kb = seq // ATTN_KB

    @pl.when(qi == 0)
    def _():
        def prep(r, carry):
            rows = pl.ds(pl.multiple_of(r * ATTN_KB, ATTN_KB), ATTN_KB)
            kh_ref[r] = _qk_prep(k_ref[rows, :], kg_ref[...], cosk_ref[rows, :],
                                 sink_ref[rows, :], seg_mean).astype(BF16)
            v1t_ref[r, 0:HEAD_W, :] = v_ref[rows, :].T.astype(BF16)
            v1t_ref[r, HEAD_W:HEAD_W + ATTN_ONES_ROWS, :] = jnp.ones((ATTN_ONES_ROWS, ATTN_KB), BF16)
            return carry

        lax.fori_loop(0, n_kb, prep, 0)

    tw = ATTN_TW
    n_units = 2 * (q_ref.shape[0] // tw)
    scale = (QK_DIM ** -0.5) * LOG2E
    for t in range(n_units // 2):
        rows = slice(t * tw, (t + 1) * tw)
        qh = _qk_prep(q_ref[rows, :], qg_ref[...], cosq_ref[rows, :], sinq_ref[rows, :], seg_mean) * scale
        lane = lax.broadcasted_iota(jnp.int32, qh.shape, 1)
        qp_ref[2 * t] = jnp.where(lane < QK_DIM, qh, 0.0).astype(BF16)
        qp_ref[2 * t + 1] = jnp.where(lane >= QK_DIM, qh, 0.0).astype(BF16)

    s_refs, p_refs = (s0_ref, s1_ref), (pa_ref, pb_ref)
    neg = jnp.full((8, tw), -jnp.inf, F32)

    def phase(us, ue, m_e):
        def sc(kb, m8):
            if us is None:
                return m8
            s_t = _dot_nt(kh_ref[kb], qp_ref[us])
            s_refs[us % 2][kb] = s_t
            return jnp.maximum(m8, jnp.max(s_t.reshape(ATTN_KB // 8, 8, tw), axis=0))

        def pr(kb, slot):
            if ue is not None:
                p_refs[slot][...] = jnp.exp2(s_refs[ue % 2][kb] - m_e).astype(BF16)

        def va(kb, slot, first=False):
            if ue is not None:
                part = _dot(v1t_ref[kb], p_refs[slot][...])
                acc_ref[ue] = part if first else acc_ref[ue] + part

        m8 = neg
        for kb in range(n_kb):
            m8 = sc(kb, m8)
            pr(kb, kb % 2)
            if kb > 0:
                va(kb - 1, (kb - 1) % 2, first=(kb == 1))
        va(n_kb - 1, (n_kb - 1) % 2)
        return jnp.max(m8, axis=0, keepdims=True)

    m = phase(0, None, None)
    for u in range(1, n_units):
        m = phase(u, u - 1, m)
    phase(None, n_units - 1, m)

    for t in range(n_units // 2):
        parts = [acc_ref[2 * t + c, 0:HEAD_W, :] / acc_ref[2 * t + c, HEAD_W:HEAD_W + 1, :]
                 for c in range(2)]
        o = (parts[0] - lam_ref[0, 0] * parts[1]).T
        o_ref[t * tw:(t + 1) * tw, :] = _rms_rows(o, sg_ref[...]) * out_scale


def diff_attention(q, k, v, cos_t, sin_t, qg, kg, sub_g, lam, lambda_init, bsz, seq, tq=ATTN_TQ):
    nq = seq // tq
    n_kb = seq // ATTN_KB
    assert n_kb >= 4 and n_kb % 2 == 0
    seg = np.kron(np.eye(2, dtype=np.float32), np.full((QK_DIM, QK_DIM), 1.0 / QK_DIM, np.float32))
    qblk = pl.BlockSpec((tq, HEAD_W), lambda b, h, i: (b * nq + i, h))
    kblk = pl.BlockSpec((seq, HEAD_W), lambda b, h, i: (b, h))
    tabq = pl.BlockSpec((tq, HEAD_W), lambda b, h, i: (i, 0))
    tabk = pl.BlockSpec((seq, HEAD_W), lambda b, h, i: (0, 0))
    vec = pl.BlockSpec((1, HEAD_W), lambda b, h, i: (0, 0))
    return pl.pallas_call(
        functools.partial(_attn_kernel, seq=seq, out_scale=1.0 - lambda_init),
        grid=(bsz, N_HEADS, nq),
        in_specs=[qblk, kblk, kblk, tabq, tabq, tabk, tabk, vec, vec, vec,
                  pl.BlockSpec(memory_space=pltpu.SMEM),
                  pl.BlockSpec((HEAD_W, HEAD_W), lambda b, h, i: (0, 0))],
        out_specs=qblk,
        out_shape=jax.ShapeDtypeStruct((bsz * seq, MIX_W), F32),
        scratch_shapes=[pltpu.VMEM((n_kb, ATTN_KB, HEAD_W), BF16),
                        pltpu.VMEM((n_kb, HEAD_W + ATTN_ONES_ROWS, ATTN_KB), BF16),
                        pltpu.VMEM((2 * (tq // ATTN_TW), ATTN_TW, HEAD_W), BF16),
                        pltpu.VMEM((n_kb, ATTN_KB, ATTN_TW), F32),
                        pltpu.VMEM((n_kb, ATTN_KB, ATTN_TW), F32),
                        pltpu.VMEM((ATTN_KB, ATTN_TW), BF16),
                        pltpu.VMEM((ATTN_KB, ATTN_TW), BF16),
                        pltpu.VMEM((2 * (tq // ATTN_TW), HEAD_W + ATTN_ONES_ROWS, ATTN_TW), F32)],
        compiler_params=_cparams(("parallel", "parallel", "arbitrary")),
        name="diff_attention",
    )(q, k, v, cos_t, sin_t, cos_t, sin_t,
      jnp.tile(qg, 2).reshape(1, HEAD_W), jnp.tile(kg, 2).reshape(1, HEAD_W),
      sub_g.reshape(1, HEAD_W), jnp.reshape(lam, (1, 1)).astype(F32), jnp.asarray(seg, BF16))


def _rope_tables(seq):
    inv = 1.0 / (ROPE_THETA ** (jnp.arange(0, QK_DIM, 2, dtype=F32) / QK_DIM))
    ang = jnp.arange(seq, dtype=F32)[:, None] * inv[None, :]
    cos, sin = jnp.cos(ang), jnp.sin(ang)
    return jnp.tile(cos, (1, 4)), jnp.tile(jnp.concatenate([-sin, sin], axis=-1), (1, 2))


def kernel(x, norm_mix_g, norm_mlp_g, w_in_even, w_out_even, sgu_norm_g, sgu_w, sgu_b,
           hgrn_lb_logits, hgrn_norm_g, w_in_odd, w_out_odd, conv_w, q_norm_g, k_norm_g,
           lambda_q1, lambda_k1, lambda_q2, lambda_k2, diff_norm_g, mlp_w1, mlp_w2):
    bsz, seq, d = x.shape
    depth = norm_mix_g.shape[0]
    cos_t, sin_t = _rope_tables(seq)
    p_lb = jax.nn.softmax(hgrn_lb_logits.astype(F32), axis=1)
    lower_bounds = jnp.cumsum(p_lb, axis=1) - p_lb[:, :1]

    x2 = x.reshape(bsz * seq, d)
    for l in range(depth):
        if l % 2 == 0:
            e = l // 2
            u, v, q, i, g, a_fwd, a_bwd = norm_proj(x2, norm_mix_g[l], w_in_even[e].astype(BF16))
            b_full = jnp.broadcast_to(sgu_b[e][:, :, None], (N_HEADS, SGU_CHUNK, HEAD_W))
            out_a = sgu(u, v, sgu_norm_g[e], sgu_w[e].astype(BF16), b_full)
            out_b = hgrn2(q, i, g, a_fwd, a_bwd, lower_bounds[:, e], hgrn_norm_g[e], bsz, seq)
            x2 = out_proj(x2, out_a, out_b, w_out_even[e].astype(BF16))
        else:
            o = l // 2
            h_in, b_gate, c_gate, q, k, v = norm_proj(x2, norm_mix_g[l], w_in_odd[o].astype(BF16))
            out_c = short_conv(h_in, b_gate, c_gate, conv_w[o], bsz, seq)
            lambda_init = 0.8 - 0.6 * math.exp(-0.3 * l)
            lam = (jnp.exp(jnp.sum(lambda_q1[o] * lambda_k1[o]))
                   - jnp.exp(jnp.sum(lambda_q2[o] * lambda_k2[o])) + lambda_init)
            out_d = diff_attention(q, k, v, cos_t, sin_t, q_norm_g[o], k_norm_g[o], diff_norm_g[o],
                                   lam, lambda_init, bsz, seq)
            x2 = out_proj(x2, out_c, out_d, w_out_odd[o].astype(BF16))
        x2 = mlp(x2, norm_mlp_g[l], mlp_w1[l].astype(BF16), mlp_w2[l].astype(BF16))
    return x2.reshape(bsz, seq, d)
```

```python
import functools
import math

import numpy as np
import jax
import jax.numpy as jnp
from jax import lax
from jax.experimental import pallas as pl
from jax.experimental.pallas import tpu as pltpu

F32 = jnp.float32
BF16 = jnp.bfloat16

EPS = 1e-6
ROPE_THETA = 10000.0
LOG2E = 1.4426950408889634

HEAD_W = 128
N_HEADS = 4
MIX_W = HEAD_W * N_HEADS
SGU_CHUNK = 128
HGRN_CHUNK = 128
QK_DIM = 64

VMEM_LIMIT = 56 * 1024 * 1024


def _cparams(sem):
    return pltpu.CompilerParams(dimension_semantics=sem, vmem_limit_bytes=VMEM_LIMIT)


def _dot(a, b):
    return jnp.dot(a, b, preferred_element_type=F32)


def _dot_nt(a, b):
    return lax.dot_general(a, b, (((1,), (1,)), ((), ())), preferred_element_type=F32)


def _split2(x):
    hi = x.astype(BF16)
    lo = (x - hi.astype(F32)).astype(BF16)
    return hi, lo


def _rms_rows(x, g):
    ms = jnp.mean(x * x, axis=-1, keepdims=True)
    return x * lax.rsqrt(ms + EPS) * g


def _norm_proj_kernel(x_ref, g_ref, w_ref, *o_refs):
    h = _rms_rows(x_ref[...], g_ref[...]).astype(BF16)
    for s, o_ref in enumerate(o_refs):
        o_ref[...] = _dot(h, w_ref[:, s * MIX_W:(s + 1) * MIX_W])


def norm_proj(x2, g, w_bf, tm=512):
    t, d = x2.shape
    n = w_bf.shape[1]
    n_split = n // MIX_W
    return pl.pallas_call(
        _norm_proj_kernel,
        grid=(t // tm,),
        in_specs=[
            pl.BlockSpec((tm, d), lambda i: (i, 0)),
            pl.BlockSpec((1, d), lambda i: (0, 0)),
            pl.BlockSpec((d, n), lambda i: (0, 0)),
        ],
        out_specs=[pl.BlockSpec((tm, MIX_W), lambda i: (i, 0)) for _ in range(n_split)],
        out_shape=[jax.ShapeDtypeStruct((t, MIX_W), F32) for _ in range(n_split)],
        compiler_params=_cparams(("parallel",)),
        name="norm_proj",
    )(x2, g.reshape(1, d), w_bf)


def _head_norm_gate(o, g, ng):
    outs = []
    for h in range(N_HEADS):
        cols = slice(h * HEAD_W, (h + 1) * HEAD_W)
        gh = g[:, cols]
        outs.append(_rms_rows(o[:, cols], ng[:, cols]) * (gh * (1.0 / (1.0 + jnp.exp(-gh)))))
    return jnp.concatenate(outs, axis=-1)


def _mix_mlp_kernel(*refs, gated):
    if gated:
        x_ref, a_ref, b_ref, gate_ref, ng_ref, wo_ref, g_ref, w1_ref, w2_ref, o_ref, h_ref, acc_ref = refs
    else:
        x_ref, a_ref, b_ref, wo_ref, g_ref, w1_ref, w2_ref, o_ref, h_ref, acc_ref = refs
    j = pl.program_id(1)

    @pl.when(j == 0)
    def _():
        b = b_ref[...]
        if gated:
            b = _head_norm_gate(b, gate_ref[...], ng_ref[...])
        x1 = x_ref[...] + _dot(a_ref[...].astype(BF16), wo_ref[:MIX_W, :])
        x1 += _dot(b.astype(BF16), wo_ref[MIX_W:, :])
        acc_ref[...] = x1
        h_ref[...] = _rms_rows(x1, g_ref[...]).astype(BF16)

    hm = _dot(h_ref[...], w1_ref[...])
    act = jnp.square(jnp.maximum(hm, 0.0)).astype(BF16)
    acc_ref[...] += _dot(act, w2_ref[...])

    @pl.when(j == pl.num_programs(1) - 1)
    def _():
        o_ref[...] = acc_ref[...]


def mix_mlp(x2, a, b, wo_bf, g, w1_bf, w2_bf, gate=None, gate_norm_g=None, tm=1024, tf=512):
    t, d = x2.shape
    ff = w1_bf.shape[1]
    row = pl.BlockSpec((tm, d), lambda i, j: (i, 0))
    half = pl.BlockSpec((tm, MIX_W), lambda i, j: (i, 0))
    gated = gate is not None
    ins = [x2, a, b] + ([gate, gate_norm_g.reshape(1, MIX_W)] if gated else [])
    specs = [row, half, half] + ([half, pl.BlockSpec((1, MIX_W), lambda i, j: (0, 0))] if gated else [])
    return pl.pallas_call(
        functools.partial(_mix_mlp_kernel, gated=gated),
        grid=(t // tm, ff // tf),
        in_specs=specs + [
            pl.BlockSpec((2 * MIX_W, d), lambda i, j: (0, 0)),
            pl.BlockSpec((1, d), lambda i, j: (0, 0)),
            pl.BlockSpec((d, tf), lambda i, j: (0, j)),
            pl.BlockSpec((tf, d), lambda i, j: (j, 0)),
        ],
        out_specs=row,
        out_shape=jax.ShapeDtypeStruct((t, d), F32),
        scratch_shapes=[pltpu.VMEM((tm, d), BF16), pltpu.VMEM((tm, d), F32)],
        compiler_params=_cparams(("parallel", "arbitrary")),
        name="mix_mlp",
    )(*ins, wo_bf, g.reshape(1, d), w1_bf, w2_bf)


def _sgu_kernel(u_ref, v_ref, g_ref, w_ref, b_ref, o_ref, *, n_chunks):
    for c in range(n_chunks):
        rows = slice(c * SGU_CHUNK, (c + 1) * SGU_CHUNK)
        for h in range(N_HEADS):
            cols = slice(h * HEAD_W, (h + 1) * HEAD_W)
            vn = _rms_rows(jax.nn.gelu(v_ref[rows, cols]), g_ref[:, cols])
            mixed = _dot(w_ref[h], vn.astype(BF16)) + b_ref[h]
            o_ref[rows, cols] = jax.nn.gelu(u_ref[rows, cols]) * mixed


def sgu(u, v, norm_g, w_s_bf, b_full, ts=512):
    t = u.shape[0]
    return pl.pallas_call(
        functools.partial(_sgu_kernel, n_chunks=ts // SGU_CHUNK),
        grid=(t // ts,),
        in_specs=[
            pl.BlockSpec((ts, MIX_W), lambda i: (i, 0)),
            pl.BlockSpec((ts, MIX_W), lambda i: (i, 0)),
            pl.BlockSpec((1, MIX_W), lambda i: (0, 0)),
            pl.BlockSpec((N_HEADS, SGU_CHUNK, SGU_CHUNK), lambda i: (0, 0, 0)),
            pl.BlockSpec((N_HEADS, SGU_CHUNK, HEAD_W), lambda i: (0, 0, 0)),
        ],
        out_specs=pl.BlockSpec((ts, MIX_W), lambda i: (i, 0)),
        out_shape=jax.ShapeDtypeStruct((t, MIX_W), F32),
        compiler_params=_cparams(("parallel",)),
        name="sgu",
    )(u, v, norm_g.reshape(1, MIX_W), w_s_bf, b_full)


SUBLANES = 8


def _hgrn_tables(c):
    levels = int(math.log2(c))
    assert 1 << levels == c
    pos = np.arange(c)
    t = pos[:, None]
    i = pos[None, :]
    d_rows, m_rows = [], []
    for lv in range(levels):
        h = c >> (lv + 1)
        start = (t // (2 * h)) * (2 * h)
        mid = start + h - 1
        upper = (t - start) >= h
        if h < SUBLANES:
            d_rows.append(np.where(upper, (i > mid) & (i <= t), (i > t) & (i <= mid)))
        same = (t // (2 * h)) == (i // (2 * h))
        m_rows.append(same & upper & ((i % (2 * h)) < h))
    n_fine = len(d_rows)
    d_rows.append(i <= t)
    d_rows.append(i > t)
    d = np.concatenate(d_rows, axis=0).astype(np.float32)
    m = np.concatenate(m_rows, axis=0).astype(np.float32)
    return d, m, levels, n_fine


def _flip_blocks(a, c):
    blocks = a.reshape(-1, c, c)
    return blocks[:, ::-1, ::-1].reshape(a.shape)


def _coarse_level(qh, kh, inc, h, fwd):
    c = inc.shape[0]
    zeros = jnp.zeros((h, inc.shape[1]), F32)
    q_rows, k_rows = [], []
    for n in range(c // (2 * h)):
        lo = slice(n * 2 * h, n * 2 * h + h)
        up = slice(n * 2 * h + h, (n + 1) * 2 * h)
        if fwd:
            ref = inc[n * 2 * h + h - 1:n * 2 * h + h, :]
            q_rows += [zeros, qh[up] * jnp.exp2(inc[up] - ref)]
            k_rows += [kh[lo] * jnp.exp2(ref - inc[lo]), zeros]
        else:
            ref = inc[n * 2 * h + h:n * 2 * h + h + 1, :]
            q_rows += [qh[lo] * jnp.exp2(inc[lo] - ref), zeros]
            k_rows += [zeros, kh[up] * jnp.exp2(ref - inc[up])]
    return jnp.concatenate(q_rows, axis=0), jnp.concatenate(k_rows, axis=0)


def _hgrn_direction(q, v_bf, vt_bf, a, lb, d_ref, m_ref, st_ref, *, levels, n_fine, fwd):
    c = q.shape[0]
    f = lb + (1.0 - lb) * (1.0 / (1.0 + jnp.exp(-a)))
    lf = jnp.log(f) * LOG2E
    kk = 1.0 - f
    hi, lo = _split2(lf)
    sums = _dot(d_ref[...], hi) + _dot(d_ref[...], lo)
    e_all = jnp.exp2(sums)
    inc_rows = slice(n_fine * c, (n_fine + 1) * c)
    exc_rows = slice((n_fine + 1) * c, (n_fine + 2) * c)
    last_row = c - 1 if fwd else 0
    row = lax.broadcasted_iota(jnp.int32, (c, c), 0)
    col = lax.broadcasted_iota(jnp.int32, (c, c), 1)
    eye = row == col
    outs = []
    for h in range(N_HEADS):
        cols = slice(h * HEAD_W, (h + 1) * HEAD_W)
        qh, kh, inc = q[:, cols], kk[:, cols], sums[inc_rows, cols]
        att = jnp.where(eye, jnp.sum(qh * kh, axis=-1, keepdims=True), 0.0)
        for lv in range(levels):
            half = c >> (lv + 1)
            if half >= SUBLANES:
                ql, kl = _coarse_level(qh, kh, inc, half, fwd)
            else:
                fine = lv - (levels - n_fine)
                e_lv = e_all[fine * c:(fine + 1) * c, cols]
                ql, kl = qh * e_lv, kh * e_lv
            pair = _dot_nt(ql.astype(BF16), kl.astype(BF16))
            att += pair if lv == 0 else pair * m_ref[lv * c:(lv + 1) * c, :]
        e_inc = e_all[inc_rows, cols]
        st = st_ref[h]
        o_h = _dot(att.astype(BF16), v_bf[:, cols])
        o_h += _dot_nt((qh * e_inc).astype(BF16), st.astype(BF16))
        k_dec = (kh * e_all[exc_rows, cols]).astype(BF16)
        st_ref[h] = st * e_inc[last_row:last_row + 1, :] + _dot(vt_bf[cols, :], k_dec)
        outs.append(o_h)
    return jnp.concatenate(outs, axis=-1)


HGRN_STEP_CHUNKS = 2


def _hgrn_kernel(qf_ref, qb_ref, if_ref, ib_ref, af_ref, ab_ref, lbf_ref, lbb_ref,
                 df_ref, db_ref, mf_ref, mb_ref, o_ref, stf_ref, stb_ref, *, levels, n_fine, chunk):
    n = pl.program_id(1)
    n_steps = pl.num_programs(1)
    blk = HGRN_STEP_CHUNKS * chunk

    @pl.when(n == 0)
    def _():
        stf_ref[...] = jnp.zeros_like(stf_ref)
        stb_ref[...] = jnp.zeros_like(stb_ref)

    def run(q_ref, i_ref, a_ref, lb_ref, d_ref, m_ref, st_ref, sub, fwd):
        rows = slice(sub * chunk, (sub + 1) * chunk)
        v = i_ref[rows, :]
        return _hgrn_direction(q_ref[rows, :], v.astype(BF16), v.T.astype(BF16), a_ref[rows, :],
                               lb_ref[...], d_ref, m_ref, st_ref, levels=levels, n_fine=n_fine, fwd=fwd)

    o_f, o_b = [], [None] * HGRN_STEP_CHUNKS
    for sub in range(HGRN_STEP_CHUNKS):
        rev = HGRN_STEP_CHUNKS - 1 - sub
        o_f.append(run(qf_ref, if_ref, af_ref, lbf_ref, df_ref, mf_ref, stf_ref, sub, True))
        o_b[rev] = run(qb_ref, ib_ref, ab_ref, lbb_ref, db_ref, mb_ref, stb_ref, rev, False)
    o_f = jnp.concatenate(o_f, axis=0)
    o_b = jnp.concatenate(o_b, axis=0)
    rows_f = pl.ds(pl.multiple_of(n * blk, blk), blk)
    rows_b = pl.ds(pl.multiple_of((n_steps - 1 - n) * blk, blk), blk)

    @pl.when(2 * n < n_steps)
    def _():
        o_ref[rows_f, :] = o_f
        o_ref[rows_b, :] = o_b

    @pl.when(2 * n >= n_steps)
    def _():
        o_ref[rows_f, :] += o_f
        o_ref[rows_b, :] += o_b


def hgrn2(q, i, a_fwd, a_bwd, lb, bsz, seq, chunk=HGRN_CHUNK):
    blk = HGRN_STEP_CHUNKS * chunk
    n_steps = seq // blk
    assert n_steps % 2 == 0
    d_np, m_np, levels, n_fine = _hgrn_tables(chunk)
    d_f = jnp.asarray(d_np, BF16)
    d_b = jnp.asarray(_flip_blocks(d_np, chunk), BF16)
    m_f = jnp.asarray(m_np, F32)
    m_b = jnp.asarray(_flip_blocks(m_np, chunk), F32)

    fwd = pl.BlockSpec((blk, MIX_W), lambda b, n: (b * n_steps + n, 0))
    bwd = pl.BlockSpec((blk, MIX_W), lambda b, n: (b * n_steps + n_steps - 1 - n, 0))
    vec = pl.BlockSpec((1, MIX_W), lambda b, n: (0, 0))

    def table(a):
        return pl.BlockSpec(a.shape, lambda b, n: (0, 0))

    return pl.pallas_call(
        functools.partial(_hgrn_kernel, levels=levels, n_fine=n_fine, chunk=chunk),
        grid=(bsz, n_steps),
        in_specs=[fwd, bwd, fwd, bwd, fwd, bwd, vec, vec,
                  table(d_f), table(d_b), table(m_f), table(m_b)],
        out_specs=pl.BlockSpec((seq, MIX_W), lambda b, n: (b, 0)),
        out_shape=jax.ShapeDtypeStruct((bsz * seq, MIX_W), F32),
        scratch_shapes=[pltpu.VMEM((N_HEADS, HEAD_W, HEAD_W), F32),
                        pltpu.VMEM((N_HEADS, HEAD_W, HEAD_W), F32)],
        compiler_params=_cparams(("parallel", "arbitrary")),
        name="hgrn2",
    )(q, q, i, i, a_fwd, a_bwd, lb[0:1], lb[1:2], d_f, d_b, m_f, m_b)


CONV_ROWS = 512
CONV_PAD = 8


def _conv_kernel(h_ref, b_ref, c_ref, w_ref, o_ref, z_ref, *, seq):
    zeros = jnp.zeros((CONV_PAD, HEAD_W), F32)
    z_ref[0:CONV_PAD, :] = zeros
    z_ref[seq + CONV_PAD:seq + 2 * CONV_PAD, :] = zeros
    n_blocks = seq // CONV_ROWS

    def gate(r, carry):
        rows = pl.ds(pl.multiple_of(r * CONV_ROWS, CONV_ROWS), CONV_ROWS)
        z_ref[pl.ds(pl.multiple_of(r * CONV_ROWS + CONV_PAD, CONV_PAD), CONV_ROWS), :] = c_ref[rows, :] * h_ref[rows, :]
        return carry

    lax.fori_loop(0, n_blocks, gate, 0)
    w_prev, w_mid, w_next = w_ref[0:1, :], w_ref[1:2, :], w_ref[2:3, :]
    ext = CONV_ROWS + 2 * CONV_PAD

    def conv(r, carry):
        zc = z_ref[pl.ds(pl.multiple_of(r * CONV_ROWS, CONV_PAD), ext), :]
        prev = pltpu.roll(zc, 1, 0)[CONV_PAD:CONV_PAD + CONV_ROWS, :]
        nxt = pltpu.roll(zc, ext - 1, 0)[CONV_PAD:CONV_PAD + CONV_ROWS, :]
        mid = zc[CONV_PAD:CONV_PAD + CONV_ROWS, :]
        rows = pl.ds(pl.multiple_of(r * CONV_ROWS, CONV_ROWS), CONV_ROWS)
        o_ref[rows, :] = b_ref[rows, :] * (w_prev * prev + w_mid * mid + w_next * nxt)
        return carry

    lax.fori_loop(0, n_blocks, conv, 0)


def short_conv(h_in, b_gate, c_gate, conv_w, bsz, seq):
    blk = pl.BlockSpec((seq, HEAD_W), lambda b, c: (b, c))
    return pl.pallas_call(
        functools.partial(_conv_kernel, seq=seq),
        grid=(bsz, N_HEADS),
        in_specs=[blk, blk, blk, pl.BlockSpec((3, HEAD_W), lambda b, c: (0, c))],
        out_specs=blk,
        out_shape=jax.ShapeDtypeStruct((bsz * seq, MIX_W), F32),
        scratch_shapes=[pltpu.VMEM((seq + 2 * CONV_PAD, HEAD_W), F32)],
        compiler_params=_cparams(("parallel", "parallel")),
        name="short_conv",
    )(h_in, b_gate, c_gate, conv_w)


ATTN_TQ = 1024
ATTN_TW = 512


def _qk_prep(x, gain, cos, sin_signed, seg_mean):
    hi, lo = _split2(x * x)
    ms = _dot(hi, seg_mean) + _dot(lo, seg_mean)
    y = x * lax.rsqrt(ms + EPS) * gain
    lane = lax.broadcasted_iota(jnp.int32, y.shape, 1)
    half = QK_DIM // 2
    partner = jnp.where((lane % QK_DIM) < half,
                        pltpu.roll(y, HEAD_W - half, 1), pltpu.roll(y, half, 1))
    return y * cos + partner * sin_signed


ATTN_ONES_ROWS = 16
ATTN_KB = 512


def _attn_kernel(q_ref, k_ref, v_ref, cosq_ref, sinq_ref, cosk_ref, sink_ref, qg_ref, kg_ref,
                 sg_ref, lam_ref, mean_ref, o_ref, kh_ref, v1t_ref, qp_ref, s0_ref, s1_ref, pa_ref, pb_ref,
                 acc_ref, *, seq, out_scale):
    qi = pl.program_id(2)
    seg_mean = mean_ref[...]
    n_kb = seq // ATTN_KB

    @pl.when(qi == 0)
    def _():
        def prep(r, carry):
            rows = pl.ds(pl.multiple_of(r * ATTN_KB, ATTN_KB), ATTN_KB)
            kh_ref[r] = _qk_prep(k_ref[rows, :], kg_ref[...], cosk_ref[rows, :],
                                 sink_ref[rows, :], seg_mean).astype(BF16)
            v1t_ref[r, 0:HEAD_W, :] = v_ref[rows, :].T.astype(BF16)
            v1t_ref[r, HEAD_W:HEAD_W + ATTN_ONES_ROWS, :] = jnp.ones((ATTN_ONES_ROWS, ATTN_KB), BF16)
            return carry

        lax.fori_loop(0, n_kb, prep, 0)

    tw = ATTN_TW
    n_units = 2 * (q_ref.shape[0] // tw)
    scale = (QK_DIM ** -0.5) * LOG2E
    for t in range(n_units // 2):
        rows = slice(t * tw, (t + 1) * tw)
        qh = _qk_prep(q_ref[rows, :], qg_ref[...], cosq_ref[rows, :], sinq_ref[rows, :], seg_mean) * scale
        lane = lax.broadcasted_iota(jnp.int32, qh.shape, 1)
        qp_ref[2 * t] = jnp.where(lane < QK_DIM, qh, 0.0).astype(BF16)
        qp_ref[2 * t + 1] = jnp.where(lane >= QK_DIM, qh, 0.0).astype(BF16)

    s_refs, p_refs = (s0_ref, s1_ref), (pa_ref, pb_ref)
    neg = jnp.full((8, tw), -jnp.inf, F32)

    def phase(us, ue, m_e):
        def sc(kb, m8):
            if us is None:
                return m8
            s_t = _dot_nt(kh_ref[kb], qp_ref[us])
            s_refs[us % 2][kb] = s_t
            return jnp.maximum(m8, jnp.max(s_t.reshape(ATTN_KB // 8, 8, tw), axis=0))

        def pr(kb, slot):
            if ue is not None:
                p_refs[slot][...] = jnp.exp2(s_refs[ue % 2][kb] - m_e).astype(BF16)

        def va(kb, slot, first=False):
            if ue is not None:
                part = _dot(v1t_ref[kb], p_refs[slot][...])
                acc_ref[ue] = part if first else acc_ref[ue] + part

        m8 = neg
        for kb in range(n_kb):
            m8 = sc(kb, m8)
            pr(kb, kb % 2)
            if kb > 0:
                va(kb - 1, (kb - 1) % 2, first=(kb == 1))
        va(n_kb - 1, (n_kb - 1) % 2)
        return jnp.max(m8, axis=0, keepdims=True)

    m = phase(0, None, None)
    for u in range(1, n_units):
        m = phase(u, u - 1, m)
    phase(None, n_units - 1, m)

    for t in range(n_units // 2):
        parts = [acc_ref[2 * t + c, 0:HEAD_W, :] / acc_ref[2 * t + c, HEAD_W:HEAD_W + 1, :]
                 for c in range(2)]
        o = (parts[0] - lam_ref[0, 0] * parts[1]).T
        o_ref[t * tw:(t + 1) * tw, :] = _rms_rows(o, sg_ref[...]) * out_scale


def diff_attention(q, k, v, cos_t, sin_t, qg, kg, sub_g, lam, lambda_init, bsz, seq, tq=ATTN_TQ):
    nq = seq // tq
    n_kb = seq // ATTN_KB
    assert n_kb >= 4 and n_kb % 2 == 0
    seg = np.kron(np.eye(2, dtype=np.float32), np.full((QK_DIM, QK_DIM), 1.0 / QK_DIM, np.float32))
    qblk = pl.BlockSpec((tq, HEAD_W), lambda b, h, i: (b * nq + i, h))
    kblk = pl.BlockSpec((seq, HEAD_W), lambda b, h, i: (b, h))
    tabq = pl.BlockSpec((tq, HEAD_W), lambda b, h, i: (i, 0))
    tabk = pl.BlockSpec((seq, HEAD_W), lambda b, h, i: (0, 0))
    vec = pl.BlockSpec((1, HEAD_W), lambda b, h, i: (0, 0))
    return pl.pallas_call(
        functools.partial(_attn_kernel, seq=seq, out_scale=1.0 - lambda_init),
        grid=(bsz, N_HEADS, nq),
        in_specs=[qblk, kblk, kblk, tabq, tabq, tabk, tabk, vec, vec, vec,
                  pl.BlockSpec(memory_space=pltpu.SMEM),
                  pl.BlockSpec((HEAD_W, HEAD_W), lambda b, h, i: (0, 0))],
        out_specs=qblk,
        out_shape=jax.ShapeDtypeStruct((bsz * seq, MIX_W), F32),
        scratch_shapes=[pltpu.VMEM((n_kb, ATTN_KB, HEAD_W), BF16),
                        pltpu.VMEM((n_kb, HEAD_W + ATTN_ONES_ROWS, ATTN_KB), BF16),
                        pltpu.VMEM((2 * (tq // ATTN_TW), ATTN_TW, HEAD_W), BF16),
                        pltpu.VMEM((n_kb, ATTN_KB, ATTN_TW), F32),
                        pltpu.VMEM((n_kb, ATTN_KB, ATTN_TW), F32),
                        pltpu.VMEM((ATTN_KB, ATTN_TW), BF16),
                        pltpu.VMEM((ATTN_KB, ATTN_TW), BF16),
                        pltpu.VMEM((2 * (tq // ATTN_TW), HEAD_W + ATTN_ONES_ROWS, ATTN_TW), F32)],
        compiler_params=_cparams(("parallel", "parallel", "arbitrary")),
        name="diff_attention",
    )(q, k, v, cos_t, sin_t, cos_t, sin_t,
      jnp.tile(qg, 2).reshape(1, HEAD_W), jnp.tile(kg, 2).reshape(1, HEAD_W),
      sub_g.reshape(1, HEAD_W), jnp.reshape(lam, (1, 1)).astype(F32), jnp.asarray(seg, BF16))


def _rope_tables(seq):
    inv = 1.0 / (ROPE_THETA ** (jnp.arange(0, QK_DIM, 2, dtype=F32) / QK_DIM))
    ang = jnp.arange(seq, dtype=F32)[:, None] * inv[None, :]
    cos, sin = jnp.cos(ang), jnp.sin(ang)
    return jnp.tile(cos, (1, 4)), jnp.tile(jnp.concatenate([-sin, sin], axis=-1), (1, 2))


def kernel(x, norm_mix_g, norm_mlp_g, w_in_even, w_out_even, sgu_norm_g, sgu_w, sgu_b,
           hgrn_lb_logits, hgrn_norm_g, w_in_odd, w_out_odd, conv_w, q_norm_g, k_norm_g,
           lambda_q1, lambda_k1, lambda_q2, lambda_k2, diff_norm_g, mlp_w1, mlp_w2):
    bsz, seq, d = x.shape
    depth = norm_mix_g.shape[0]
    cos_t, sin_t = _rope_tables(seq)
    p_lb = jax.nn.softmax(hgrn_lb_logits.astype(F32), axis=1)
    lower_bounds = jnp.cumsum(p_lb, axis=1) - p_lb[:, :1]

    x2 = x.reshape(bsz * seq, d)
    for l in range(depth):
        if l % 2 == 0:
            e = l // 2
            u, v, q, i, g, a_fwd, a_bwd = norm_proj(x2, norm_mix_g[l], w_in_even[e].astype(BF16))
            b_full = jnp.broadcast_to(sgu_b[e][:, :, None], (N_HEADS, SGU_CHUNK, HEAD_W))
            out_a = sgu(u, v, sgu_norm_g[e], sgu_w[e].astype(BF16), b_full)
            o_sum = hgrn2(q, i, a_fwd, a_bwd, lower_bounds[:, e], bsz, seq)
            x2 = mix_mlp(x2, out_a, o_sum, w_out_even[e].astype(BF16), norm_mlp_g[l],
                         mlp_w1[l].astype(BF16), mlp_w2[l].astype(BF16),
                         gate=g, gate_norm_g=hgrn_norm_g[e])
        else:
            o = l // 2
            h_in, b_gate, c_gate, q, k, v = norm_proj(x2, norm_mix_g[l], w_in_odd[o].astype(BF16))
            out_c = short_conv(h_in, b_gate, c_gate, conv_w[o], bsz, seq)
            lambda_init = 0.8 - 0.6 * math.exp(-0.3 * l)
            lam = (jnp.exp(jnp.sum(lambda_q1[o] * lambda_k1[o]))
                   - jnp.exp(jnp.sum(lambda_q2[o] * lambda_k2[o])) + lambda_init)
            out_d = diff_attention(q, k, v, cos_t, sin_t, q_norm_g[o], k_norm_g[o], diff_norm_g[o],
                                   lam, lambda_init, bsz, seq)
            x2 = mix_mlp(x2, out_c, out_d, w_out_odd[o].astype(BF16), norm_mlp_g[l],
                         mlp_w1[l].astype(BF16), mlp_w2[l].astype(BF16))
    return x2.reshape(bsz, seq, d)
```

```python
import functools
import math

import numpy as np
import jax
import jax.numpy as jnp
from jax import lax
from jax.experimental import pallas as pl
from jax.experimental.pallas import tpu as pltpu

F32 = jnp.float32
BF16 = jnp.bfloat16

EPS = 1e-6
ROPE_THETA = 10000.0
LOG2E = 1.4426950408889634

HEAD_W = 128
N_HEADS = 4
MIX_W = HEAD_W * N_HEADS
SGU_CHUNK = 128
HGRN_CHUNK = 128
QK_DIM = 64

VMEM_LIMIT = 56 * 1024 * 1024


def _cparams(sem):
    return pltpu.CompilerParams(dimension_semantics=sem, vmem_limit_bytes=VMEM_LIMIT)


def _dot(a, b):
    return jnp.dot(a, b, preferred_element_type=F32)


def _dot_nt(a, b):
    return lax.dot_general(a, b, (((1,), (1,)), ((), ())), preferred_element_type=F32)


def _split2(x):
    hi = x.astype(BF16)
    lo = (x - hi.astype(F32)).astype(BF16)
    return hi, lo


def _rms_rows(x, g):
    ms = jnp.mean(x * x, axis=-1, keepdims=True)
    return x * lax.rsqrt(ms + EPS) * g


def _norm_proj_kernel(x_ref, g_ref, w_ref, *o_refs):
    h = _rms_rows(x_ref[...], g_ref[...]).astype(BF16)
    for s, o_ref in enumerate(o_refs):
        o_ref[...] = _dot(h, w_ref[:, s * MIX_W:(s + 1) * MIX_W])


def norm_proj(x2, g, w_bf, tm=512):
    t, d = x2.shape
    n = w_bf.shape[1]
    n_split = n // MIX_W
    return pl.pallas_call(
        _norm_proj_kernel,
        grid=(t // tm,),
        in_specs=[
            pl.BlockSpec((tm, d), lambda i: (i, 0)),
            pl.BlockSpec((1, d), lambda i: (0, 0)),
            pl.BlockSpec((d, n), lambda i: (0, 0)),
        ],
        out_specs=[pl.BlockSpec((tm, MIX_W), lambda i: (i, 0)) for _ in range(n_split)],
        out_shape=[jax.ShapeDtypeStruct((t, MIX_W), F32) for _ in range(n_split)],
        compiler_params=_cparams(("parallel",)),
        name="norm_proj",
    )(x2, g.reshape(1, d), w_bf)


def _head_norm_gate(o, g, ng):
    outs = []
    for h in range(N_HEADS):
        cols = slice(h * HEAD_W, (h + 1) * HEAD_W)
        gh = g[:, cols]
        outs.append(_rms_rows(o[:, cols], ng[:, cols]) * (gh * (1.0 / (1.0 + jnp.exp(-gh)))))
    return jnp.concatenate(outs, axis=-1)


def _sgu_block(u, v, g, w_ref, b_ref):
    outs = []
    for h in range(N_HEADS):
        cols = slice(h * HEAD_W, (h + 1) * HEAD_W)
        vn = _rms_rows(jax.nn.gelu(v[:, cols]), g[:, cols])
        mixed = _dot(w_ref[h], vn.astype(BF16)) + b_ref[h]
        outs.append(jax.nn.gelu(u[:, cols]) * mixed)
    return jnp.concatenate(outs, axis=-1)


def _mix_mlp_kernel(*refs, even, n_tiles):
    if even:
        (x_ref, u_ref, v_ref, b_ref, gate_ref, sg_ref, ws_ref, bs_ref, ng_ref,
         wo_ref, g_ref, w1_ref, w2_ref, o_ref, h0_ref, h1_ref, acc0_ref, acc1_ref) = refs
    else:
        (x_ref, a_ref, b_ref,
         wo_ref, g_ref, w1_ref, w2_ref, o_ref, h0_ref, h1_ref, acc0_ref, acc1_ref) = refs
    h_refs, acc_refs = (h0_ref, h1_ref), (acc0_ref, acc1_ref)
    i, j = pl.program_id(0), pl.program_id(1)
    rb = x_ref.shape[0]
    rows = pl.ds(pl.multiple_of(j * rb, rb), rb)

    def prepare(slot):
        if even:
            a = _sgu_block(u_ref[...], v_ref[...], sg_ref[...], ws_ref, bs_ref)
            b = _head_norm_gate(b_ref[...], gate_ref[...], ng_ref[...])
        else:
            a, b = a_ref[...], b_ref[...]
        x1 = x_ref[...] + _dot(a.astype(BF16), wo_ref[:MIX_W, :])
        x1 += _dot(b.astype(BF16), wo_ref[MIX_W:, :])
        acc_refs[slot][rows, :] = x1
        h_refs[slot][rows, :] = _rms_rows(x1, g_ref[...]).astype(BF16)

    def mlp_chunk(slot):
        hm = _dot(h_refs[slot][...], w1_ref[...])
        act = jnp.square(jnp.maximum(hm, 0.0)).astype(BF16)
        acc_refs[slot][...] += _dot(act, w2_ref[...])

    odd_step = (i % 2) == 1

    @pl.when(i == 0)
    def _():
        prepare(0)

    @pl.when((i > 0) & (i < n_tiles) & jnp.logical_not(odd_step))
    def _():
        mlp_chunk(1)
        prepare(0)

    @pl.when(odd_step)
    def _():
        mlp_chunk(0)
        prepare(1)

    @pl.when(i == n_tiles)
    def _():
        mlp_chunk(1)

    last = j == pl.num_programs(1) - 1

    @pl.when(last & odd_step)
    def _():
        o_ref[...] = acc0_ref[...]

    @pl.when(last & (i > 0) & jnp.logical_not(odd_step))
    def _():
        o_ref[...] = acc1_ref[...]


def mix_mlp(x2, mix_in, wo_bf, g, w1_bf, w2_bf, sgu_params=None, tm=1024, tf=512):
    t, d = x2.shape
    ff = w1_bf.shape[1]
    n_tiles, n_j = t // tm, ff // tf
    rb = tm // n_j
    assert n_tiles % 2 == 0 and rb == SGU_CHUNK
    even = sgu_params is not None

    def blk(width):
        return pl.BlockSpec((rb, width), lambda i, j: (jnp.minimum(i, n_tiles - 1) * n_j + j, 0))

    def const(shape):
        return pl.BlockSpec(shape, lambda i, j: (0,) * len(shape))

    ins = [x2] + list(mix_in)
    specs = [blk(d)] + [blk(MIX_W) for _ in mix_in]
    if even:
        norm_g, w_s_bf, b_full, hgrn_norm_g = sgu_params
        ins += [norm_g.reshape(1, MIX_W), w_s_bf, b_full, hgrn_norm_g.reshape(1, MIX_W)]
        specs += [const((1, MIX_W)), const((N_HEADS, SGU_CHUNK, SGU_CHUNK)),
                  const((N_HEADS, SGU_CHUNK, HEAD_W)), const((1, MIX_W))]
    return pl.pallas_call(
        functools.partial(_mix_mlp_kernel, even=even, n_tiles=n_tiles),
        grid=(n_tiles + 1, n_j),
        in_specs=specs + [
            const((2 * MIX_W, d)),
            const((1, d)),
            pl.BlockSpec((d, tf), lambda i, j: (0, j)),
            pl.BlockSpec((tf, d), lambda i, j: (j, 0)),
        ],
        out_specs=pl.BlockSpec((tm, d), lambda i, j: (jnp.maximum(i - 1, 0), 0)),
        out_shape=jax.ShapeDtypeStruct((t, d), F32),
        scratch_shapes=[pltpu.VMEM((tm, d), BF16), pltpu.VMEM((tm, d), BF16),
                        pltpu.VMEM((tm, d), F32), pltpu.VMEM((tm, d), F32)],
        compiler_params=_cparams(("arbitrary", "arbitrary")),
        name="mix_mlp",
    )(*ins, wo_bf, g.reshape(1, d), w1_bf, w2_bf)


SUBLANES = 8


def _hgrn_tables(c):
    levels = int(math.log2(c))
    assert 1 << levels == c
    pos = np.arange(c)
    t = pos[:, None]
    i = pos[None, :]
    d_rows, m_rows = [], []
    for lv in range(levels):
        h = c >> (lv + 1)
        start = (t // (2 * h)) * (2 * h)
        mid = start + h - 1
        upper = (t - start) >= h
        if h < SUBLANES:
            d_rows.append(np.where(upper, (i > mid) & (i <= t), (i > t) & (i <= mid)))
        same = (t // (2 * h)) == (i // (2 * h))
        m_rows.append(same & upper & ((i % (2 * h)) < h))
    n_fine = len(d_rows)
    d_rows.append(i <= t)
    d_rows.append(i > t)
    d = np.concatenate(d_rows, axis=0).astype(np.float32)
    m = np.concatenate(m_rows, axis=0).astype(np.float32)
    return d, m, levels, n_fine


def _flip_blocks(a, c):
    blocks = a.reshape(-1, c, c)
    return blocks[:, ::-1, ::-1].reshape(a.shape)


def _coarse_level(qh, kh, inc, h, fwd):
    c = inc.shape[0]
    zeros = jnp.zeros((h, inc.shape[1]), F32)
    q_rows, k_rows = [], []
    for n in range(c // (2 * h)):
        lo = slice(n * 2 * h, n * 2 * h + h)
        up = slice(n * 2 * h + h, (n + 1) * 2 * h)
        if fwd:
            ref = inc[n * 2 * h + h - 1:n * 2 * h + h, :]
            q_rows += [zeros, qh[up] * jnp.exp2(inc[up] - ref)]
            k_rows += [kh[lo] * jnp.exp2(ref - inc[lo]), zeros]
        else:
            ref = inc[n * 2 * h + h:n * 2 * h + h + 1, :]
            q_rows += [qh[lo] * jnp.exp2(inc[lo] - ref), zeros]
            k_rows += [zeros, kh[up] * jnp.exp2(ref - inc[up])]
    return jnp.concatenate(q_rows, axis=0), jnp.concatenate(k_rows, axis=0)


def _hgrn_direction(q, v_bf, vt_bf, a, lb, d_ref, m_ref, st_ref, *, levels, n_fine, fwd):
    c = q.shape[0]
    f = lb + (1.0 - lb) * (1.0 / (1.0 + jnp.exp(-a)))
    lf = jnp.log(f) * LOG2E
    kk = 1.0 - f
    hi, lo = _split2(lf)
    sums = _dot(d_ref[...], hi) + _dot(d_ref[...], lo)
    e_all = jnp.exp2(sums)
    inc_rows = slice(n_fine * c, (n_fine + 1) * c)
    exc_rows = slice((n_fine + 1) * c, (n_fine + 2) * c)
    last_row = c - 1 if fwd else 0
    row = lax.broadcasted_iota(jnp.int32, (c, c), 0)
    col = lax.broadcasted_iota(jnp.int32, (c, c), 1)
    eye = row == col
    outs = []
    for h in range(N_HEADS):
        cols = slice(h * HEAD_W, (h + 1) * HEAD_W)
        qh, kh, inc = q[:, cols], kk[:, cols], sums[inc_rows, cols]
        att = jnp.where(eye, jnp.sum(qh * kh, axis=-1, keepdims=True), 0.0)
        for lv in range(levels):
            half = c >> (lv + 1)
            if half >= SUBLANES:
                ql, kl = _coarse_level(qh, kh, inc, half, fwd)
            else:
                fine = lv - (levels - n_fine)
                e_lv = e_all[fine * c:(fine + 1) * c, cols]
                ql, kl = qh * e_lv, kh * e_lv
            pair = _dot_nt(ql.astype(BF16), kl.astype(BF16))
            att += pair if lv == 0 else pair * m_ref[lv * c:(lv + 1) * c, :]
        e_inc = e_all[inc_rows, cols]
        st = st_ref[h]
        o_h = _dot(att.astype(BF16), v_bf[:, cols])
        o_h += _dot_nt((qh * e_inc).astype(BF16), st.astype(BF16))
        k_dec = (kh * e_all[exc_rows, cols]).astype(BF16)
        st_ref[h] = st * e_inc[last_row:last_row + 1, :] + _dot(vt_bf[cols, :], k_dec)
        outs.append(o_h)
    return jnp.concatenate(outs, axis=-1)


HGRN_STEP_CHUNKS = 2


def _hgrn_kernel(qf_ref, qb_ref, if_ref, ib_ref, af_ref, ab_ref, lbf_ref, lbb_ref,
                 df_ref, db_ref, mf_ref, mb_ref, o_ref, stf_ref, stb_ref, *, levels, n_fine, chunk):
    n = pl.program_id(1)
    n_steps = pl.num_programs(1)
    blk = HGRN_STEP_CHUNKS * chunk

    @pl.when(n == 0)
    def _():
        stf_ref[...] = jnp.zeros_like(stf_ref)
        stb_ref[...] = jnp.zeros_like(stb_ref)

    def run(q_ref, i_ref, a_ref, lb_ref, d_ref, m_ref, st_ref, sub, fwd):
        rows = slice(sub * chunk, (sub + 1) * chunk)
        v = i_ref[rows, :]
        return _hgrn_direction(q_ref[rows, :], v.astype(BF16), v.T.astype(BF16), a_ref[rows, :],
                               lb_ref[...], d_ref, m_ref, st_ref, levels=levels, n_fine=n_fine, fwd=fwd)

    o_f, o_b = [], [None] * HGRN_STEP_CHUNKS
    for sub in range(HGRN_STEP_CHUNKS):
        rev = HGRN_STEP_CHUNKS - 1 - sub
        o_f.append(run(qf_ref, if_ref, af_ref, lbf_ref, df_ref, mf_ref, stf_ref, sub, True))
        o_b[rev] = run(qb_ref, ib_ref, ab_ref, lbb_ref, db_ref, mb_ref, stb_ref, rev, False)
    o_f = jnp.concatenate(o_f, axis=0)
    o_b = jnp.concatenate(o_b, axis=0)
    rows_f = pl.ds(pl.multiple_of(n * blk, blk), blk)
    rows_b = pl.ds(pl.multiple_of((n_steps - 1 - n) * blk, blk), blk)

    @pl.when(2 * n < n_steps)
    def _():
        o_ref[rows_f, :] = o_f
        o_ref[rows_b, :] = o_b

    @pl.when(2 * n >= n_steps)
    def _():
        o_ref[rows_f, :] += o_f
        o_ref[rows_b, :] += o_b


def hgrn2(q, i, a_fwd, a_bwd, lb, bsz, seq, chunk=HGRN_CHUNK):
    blk = HGRN_STEP_CHUNKS * chunk
    n_steps = seq // blk
    assert n_steps % 2 == 0
    d_np, m_np, levels, n_fine = _hgrn_tables(chunk)
    d_f = jnp.asarray(d_np, BF16)
    d_b = jnp.asarray(_flip_blocks(d_np, chunk), BF16)
    m_f = jnp.asarray(m_np, F32)
    m_b = jnp.asarray(_flip_blocks(m_np, chunk), F32)

    fwd = pl.BlockSpec((blk, MIX_W), lambda b, n: (b * n_steps + n, 0))
    bwd = pl.BlockSpec((blk, MIX_W), lambda b, n: (b * n_steps + n_steps - 1 - n, 0))
    vec = pl.BlockSpec((1, MIX_W), lambda b, n: (0, 0))

    def table(a):
        return pl.BlockSpec(a.shape, lambda b, n: (0, 0))

    return pl.pallas_call(
        functools.partial(_hgrn_kernel, levels=levels, n_fine=n_fine, chunk=chunk),
        grid=(bsz, n_steps),
        in_specs=[fwd, bwd, fwd, bwd, fwd, bwd, vec, vec,
                  table(d_f), table(d_b), table(m_f), table(m_b)],
        out_specs=pl.BlockSpec((seq, MIX_W), lambda b, n: (b, 0)),
        out_shape=jax.ShapeDtypeStruct((bsz * seq, MIX_W), F32),
        scratch_shapes=[pltpu.VMEM((N_HEADS, HEAD_W, HEAD_W), F32),
                        pltpu.VMEM((N_HEADS, HEAD_W, HEAD_W), F32)],
        compiler_params=_cparams(("parallel", "arbitrary")),
        name="hgrn2",
    )(q, q, i, i, a_fwd, a_bwd, lb[0:1], lb[1:2], d_f, d_b, m_f, m_b)


CONV_ROWS = 512
CONV_PAD = 8


def _conv_kernel(h_ref, b_ref, c_ref, w_ref, o_ref, z_ref, *, seq):
    zeros = jnp.zeros((CONV_PAD, HEAD_W), F32)
    z_ref[0:CONV_PAD, :] = zeros
    z_ref[seq + CONV_PAD:seq + 2 * CONV_PAD, :] = zeros
    n_blocks = seq // CONV_ROWS

    def gate(r, carry):
        rows = pl.ds(pl.multiple_of(r * CONV_ROWS, CONV_ROWS), CONV_ROWS)
        z_ref[pl.ds(pl.multiple_of(r * CONV_ROWS + CONV_PAD, CONV_PAD), CONV_ROWS), :] = c_ref[rows, :] * h_ref[rows, :]
        return carry

    lax.fori_loop(0, n_blocks, gate, 0)
    w_prev, w_mid, w_next = w_ref[0:1, :], w_ref[1:2, :], w_ref[2:3, :]
    ext = CONV_ROWS + 2 * CONV_PAD

    def conv(r, carry):
        zc = z_ref[pl.ds(pl.multiple_of(r * CONV_ROWS, CONV_PAD), ext), :]
        prev = pltpu.roll(zc, 1, 0)[CONV_PAD:CONV_PAD + CONV_ROWS, :]
        nxt = pltpu.roll(zc, ext - 1, 0)[CONV_PAD:CONV_PAD + CONV_ROWS, :]
        mid = zc[CONV_PAD:CONV_PAD + CONV_ROWS, :]
        rows = pl.ds(pl.multiple_of(r * CONV_ROWS, CONV_ROWS), CONV_ROWS)
        o_ref[rows, :] = b_ref[rows, :] * (w_prev * prev + w_mid * mid + w_next * nxt)
        return carry

    lax.fori_loop(0, n_blocks, conv, 0)


def short_conv(h_in, b_gate, c_gate, conv_w, bsz, seq):
    blk = pl.BlockSpec((seq, HEAD_W), lambda b, c: (b, c))
    return pl.pallas_call(
        functools.partial(_conv_kernel, seq=seq),
        grid=(bsz, N_HEADS),
        in_specs=[blk, blk, blk, pl.BlockSpec((3, HEAD_W), lambda b, c: (0, c))],
        out_specs=blk,
        out_shape=jax.ShapeDtypeStruct((bsz * seq, MIX_W), F32),
        scratch_shapes=[pltpu.VMEM((seq + 2 * CONV_PAD, HEAD_W), F32)],
        compiler_params=_cparams(("parallel", "parallel")),
        name="short_conv",
    )(h_in, b_gate, c_gate, conv_w)


ATTN_TQ = 1024
ATTN_TW = 512


def _qk_prep(x, gain, cos, sin_signed, seg_mean):
    hi, lo = _split2(x * x)
    ms = _dot(hi, seg_mean) + _dot(lo, seg_mean)
    y = x * lax.rsqrt(ms + EPS) * gain
    lane = lax.broadcasted_iota(jnp.int32, y.shape, 1)
    half = QK_DIM // 2
    partner = jnp.where((lane % QK_DIM) < half,
                        pltpu.roll(y, HEAD_W - half, 1), pltpu.roll(y, half, 1))
    return y * cos + partner * sin_signed


ATTN_ONES_ROWS = 16
ATTN_KB = 512


def _attn_kernel(q_ref, k_ref, v_ref, cosq_ref, sinq_ref, cosk_ref, sink_ref, qg_ref, kg_ref,
                 sg_ref, lam_ref, mean_ref, o_ref, kh_ref, v1t_ref, qp_ref, s0_ref, s1_ref, pa_ref, pb_ref,
                 acc_ref, *, seq, out_scale):
    qi = pl.program_id(2)
    seg_mean = mean_ref[...]
    n_kb = seq // ATTN_KB

    @pl.when(qi == 0)
    def _():
        def prep(r, carry):
            rows = pl.ds(pl.multiple_of(r * ATTN_KB, ATTN_KB), ATTN_KB)
            kh_ref[r] = _qk_prep(k_ref[rows, :], kg_ref[...], cosk_ref[rows, :],
                                 sink_ref[rows, :], seg_mean).astype(BF16)
            v1t_ref[r, 0:HEAD_W, :] = v_ref[rows, :].T.astype(BF16)
            v1t_ref[r, HEAD_W:HEAD_W + ATTN_ONES_ROWS, :] = jnp.ones((ATTN_ONES_ROWS, ATTN_KB), BF16)
            return carry

        lax.fori_loop(0, n_kb, prep, 0)

    tw = ATTN_TW
    n_units = 2 * (q_ref.shape[0] // tw)
    scale = (QK_DIM ** -0.5) * LOG2E
    for t in range(n_units // 2):
        rows = slice(t * tw, (t + 1) * tw)
        qh = _qk_prep(q_ref[rows, :], qg_ref[...], cosq_ref[rows, :], sinq_ref[rows, :], seg_mean) * scale
        lane = lax.broadcasted_iota(jnp.int32, qh.shape, 1)
        qp_ref[2 * t] = jnp.where(lane < QK_DIM, qh, 0.0).astype(BF16)
        qp_ref[2 * t + 1] = jnp.where(lane >= QK_DIM, qh, 0.0).astype(BF16)

    s_refs, p_refs = (s0_ref, s1_ref), (pa_ref, pb_ref)
    neg = jnp.full((8, tw), -jnp.inf, F32)

    def phase(us, ue, m_e):
        def sc(kb, m8):
            if us is None:
                return m8
            s_t = _dot_nt(kh_ref[kb], qp_ref[us])
            s_refs[us % 2][kb] = s_t
            return jnp.maximum(m8, jnp.max(s_t.reshape(ATTN_KB // 8, 8, tw), axis=0))

        def pr(kb, slot):
            if ue is not None:
                p_refs[slot][...] = jnp.exp2(s_refs[ue % 2][kb] - m_e).astype(BF16)

        def va(kb, slot, first=False):
            if ue is not None:
                part = _dot(v1t_ref[kb], p_refs[slot][...])
                acc_ref[ue] = part if first else acc_ref[ue] + part

        m8 = neg
        for kb in range(n_kb):
            m8 = sc(kb, m8)
            pr(kb, kb % 2)
            if kb > 0:
                va(kb - 1, (kb - 1) % 2, first=(kb == 1))
        va(n_kb - 1, (n_kb - 1) % 2)
        return jnp.max(m8, axis=0, keepdims=True)

    m = phase(0, None, None)
    for u in range(1, n_units):
        m = phase(u, u - 1, m)
    phase(None, n_units - 1, m)

    for t in range(n_units // 2):
        parts = [acc_ref[2 * t + c, 0:HEAD_W, :] / acc_ref[2 * t + c, HEAD_W:HEAD_W + 1, :]
                 for c in range(2)]
        o = (parts[0] - lam_ref[0, 0] * parts[1]).T
        o_ref[t * tw:(t + 1) * tw, :] = _rms_rows(o, sg_ref[...]) * out_scale


def diff_attention(q, k, v, cos_t, sin_t, qg, kg, sub_g, lam, lambda_init, bsz, seq, tq=ATTN_TQ):
    nq = seq // tq
    n_kb = seq // ATTN_KB
    assert n_kb >= 4 and n_kb % 2 == 0
    seg = np.kron(np.eye(2, dtype=np.float32), np.full((QK_DIM, QK_DIM), 1.0 / QK_DIM, np.float32))
    qblk = pl.BlockSpec((tq, HEAD_W), lambda b, h, i: (b * nq + i, h))
    kblk = pl.BlockSpec((seq, HEAD_W), lambda b, h, i: (b, h))
    tabq = pl.BlockSpec((tq, HEAD_W), lambda b, h, i: (i, 0))
    tabk = pl.BlockSpec((seq, HEAD_W), lambda b, h, i: (0, 0))
    vec = pl.BlockSpec((1, HEAD_W), lambda b, h, i: (0, 0))
    return pl.pallas_call(
        functools.partial(_attn_kernel, seq=seq, out_scale=1.0 - lambda_init),
        grid=(bsz, N_HEADS, nq),
        in_specs=[qblk, kblk, kblk, tabq, tabq, tabk, tabk, vec, vec, vec,
                  pl.BlockSpec(memory_space=pltpu.SMEM),
                  pl.BlockSpec((HEAD_W, HEAD_W), lambda b, h, i: (0, 0))],
        out_specs=qblk,
        out_shape=jax.ShapeDtypeStruct((bsz * seq, MIX_W), F32),
        scratch_shapes=[pltpu.VMEM((n_kb, ATTN_KB, HEAD_W), BF16),
                        pltpu.VMEM((n_kb, HEAD_W + ATTN_ONES_ROWS, ATTN_KB), BF16),
                        pltpu.VMEM((2 * (tq // ATTN_TW), ATTN_TW, HEAD_W), BF16),
                        pltpu.VMEM((n_kb, ATTN_KB, ATTN_TW), F32),
                        pltpu.VMEM((n_kb, ATTN_KB, ATTN_TW), F32),
                        pltpu.VMEM((ATTN_KB, ATTN_TW), BF16),
                        pltpu.VMEM((ATTN_KB, ATTN_TW), BF16),
                        pltpu.VMEM((2 * (tq // ATTN_TW), HEAD_W + ATTN_ONES_ROWS, ATTN_TW), F32)],
        compiler_params=_cparams(("parallel", "parallel", "arbitrary")),
        name="diff_attention",
    )(q, k, v, cos_t, sin_t, cos_t, sin_t,
      jnp.tile(qg, 2).reshape(1, HEAD_W), jnp.tile(kg, 2).reshape(1, HEAD_W),
      sub_g.reshape(1, HEAD_W), jnp.reshape(lam, (1, 1)).astype(F32), jnp.asarray(seg, BF16))


def _rope_tables(seq):
    inv = 1.0 / (ROPE_THETA ** (jnp.arange(0, QK_DIM, 2, dtype=F32) / QK_DIM))
    ang = jnp.arange(seq, dtype=F32)[:, None] * inv[None, :]
    cos, sin = jnp.cos(ang), jnp.sin(ang)
    return jnp.tile(cos, (1, 4)), jnp.tile(jnp.concatenate([-sin, sin], axis=-1), (1, 2))


def kernel(x, norm_mix_g, norm_mlp_g, w_in_even, w_out_even, sgu_norm_g, sgu_w, sgu_b,
           hgrn_lb_logits, hgrn_norm_g, w_in_odd, w_out_odd, conv_w, q_norm_g, k_norm_g,
           lambda_q1, lambda_k1, lambda_q2, lambda_k2, diff_norm_g, mlp_w1, mlp_w2):
    bsz, seq, d = x.shape
    depth = norm_mix_g.shape[0]
    cos_t, sin_t = _rope_tables(seq)
    p_lb = jax.nn.softmax(hgrn_lb_logits.astype(F32), axis=1)
    lower_bounds = jnp.cumsum(p_lb, axis=1) - p_lb[:, :1]

    x2 = x.reshape(bsz * seq, d)
    for l in range(depth):
        if l % 2 == 0:
            e = l // 2
            u, v, q, i, g, a_fwd, a_bwd = norm_proj(x2, norm_mix_g[l], w_in_even[e].astype(BF16))
            b_full = jnp.broadcast_to(sgu_b[e][:, :, None], (N_HEADS, SGU_CHUNK, HEAD_W))
            o_sum = hgrn2(q, i, a_fwd, a_bwd, lower_bounds[:, e], bsz, seq)
            x2 = mix_mlp(x2, (u, v, o_sum, g), w_out_even[e].astype(BF16), norm_mlp_g[l],
                         mlp_w1[l].astype(BF16), mlp_w2[l].astype(BF16),
                         sgu_params=(sgu_norm_g[e], sgu_w[e].astype(BF16), b_full, hgrn_norm_g[e]))
        else:
            o = l // 2
            h_in, b_gate, c_gate, q, k, v = norm_proj(x2, norm_mix_g[l], w_in_odd[o].astype(BF16))
            out_c = short_conv(h_in, b_gate, c_gate, conv_w[o], bsz, seq)
            lambda_init = 0.8 - 0.6 * math.exp(-0.3 * l)
            lam = (jnp.exp(jnp.sum(lambda_q1[o] * lambda_k1[o]))
                   - jnp.exp(jnp.sum(lambda_q2[o] * lambda_k2[o])) + lambda_init)
            out_d = diff_attention(q, k, v, cos_t, sin_t, q_norm_g[o], k_norm_g[o], diff_norm_g[o],
                                   lam, lambda_init, bsz, seq)
            x2 = mix_mlp(x2, (out_c, out_d), w_out_odd[o].astype(BF16), norm_mlp_g[l],
                         mlp_w1[l].astype(BF16), mlp_w2[l].astype(BF16))
    return x2.reshape(bsz, seq, d)
```

```python
import functools
import math

import numpy as np
import jax
import jax.numpy as jnp
from jax import lax
from jax.experimental import pallas as pl
from jax.experimental.pallas import tpu as pltpu

F32 = jnp.float32
BF16 = jnp.bfloat16

EPS = 1e-6
ROPE_THETA = 10000.0
LOG2E = 1.4426950408889634

HEAD_W = 128
N_HEADS = 4
MIX_W = HEAD_W * N_HEADS
SGU_CHUNK = 128
HGRN_CHUNK = 128
QK_DIM = 64

VMEM_LIMIT = 56 * 1024 * 1024


def _cparams(sem):
    return pltpu.CompilerParams(dimension_semantics=sem, vmem_limit_bytes=VMEM_LIMIT)


def _dot(a, b):
    return jnp.dot(a, b, preferred_element_type=F32)


def _dot_nt(a, b):
    return lax.dot_general(a, b, (((1,), (1,)), ((), ())), preferred_element_type=F32)


def _split2(x):
    hi = x.astype(BF16)
    lo = (x - hi.astype(F32)).astype(BF16)
    return hi, lo


def _rms_rows(x, g):
    ms = jnp.mean(x * x, axis=-1, keepdims=True)
    return x * lax.rsqrt(ms + EPS) * g


def _norm_proj_kernel(x_ref, g_ref, w_ref, *rest):
    o_refs, wbf_ref = rest[:-1], rest[-1]

    @pl.when(pl.program_id(0) == 0)
    def _():
        for s in range(len(o_refs)):
            cols = slice(s * MIX_W, (s + 1) * MIX_W)
            wbf_ref[:, cols] = w_ref[:, cols].astype(BF16)

    h = _rms_rows(x_ref[...], g_ref[...]).astype(BF16)
    for s, o_ref in enumerate(o_refs):
        o_ref[...] = _dot(h, wbf_ref[:, s * MIX_W:(s + 1) * MIX_W])


def norm_proj(x2, g, w_stack, layer, tm=512):
    t, d = x2.shape
    n = w_stack.shape[2]
    n_split = n // MIX_W
    return pl.pallas_call(
        _norm_proj_kernel,
        grid=(t // tm,),
        in_specs=[
            pl.BlockSpec((tm, d), lambda i: (i, 0)),
            pl.BlockSpec((1, d), lambda i: (0, 0)),
            pl.BlockSpec((None, d, n), lambda i: (layer, 0, 0), pipeline_mode=pl.Buffered(1)),
        ],
        out_specs=[pl.BlockSpec((tm, MIX_W), lambda i: (i, 0)) for _ in range(n_split)],
        out_shape=[jax.ShapeDtypeStruct((t, MIX_W), F32) for _ in range(n_split)],
        scratch_shapes=[pltpu.VMEM((d, n), BF16)],
        compiler_params=_cparams(("arbitrary",)),
        name="norm_proj",
    )(x2, g.reshape(1, d), w_stack)


def _head_norm_gate(o, g, ng):
    outs = []
    for h in range(N_HEADS):
        cols = slice(h * HEAD_W, (h + 1) * HEAD_W)
        gh = g[:, cols]
        outs.append(_rms_rows(o[:, cols], ng[:, cols]) * (gh * (1.0 / (1.0 + jnp.exp(-gh)))))
    return jnp.concatenate(outs, axis=-1)


def _sgu_block(u, v, g, w_ref, b_ref):
    outs = []
    for h in range(N_HEADS):
        cols = slice(h * HEAD_W, (h + 1) * HEAD_W)
        vn = _rms_rows(jax.nn.gelu(v[:, cols]), g[:, cols])
        mixed = _dot(w_ref[h], vn.astype(BF16)) + b_ref[h]
        outs.append(jax.nn.gelu(u[:, cols]) * mixed)
    return jnp.concatenate(outs, axis=-1)


CONV_HALO = 8


def _mix_mlp_kernel(*refs, even, n_tiles, seq):
    if even:
        (x_ref, u_ref, v_ref, b_ref, gate_ref, sg_ref, ws_ref, bs_ref, ng_ref,
         wo_ref, g_ref, w1_ref, w2_ref, o_ref, h0_ref, h1_ref, acc0_ref, acc1_ref) = refs
    else:
        (x_ref, hin_ref, bg_ref, cg_ref, b_ref, hprev_ref, cprev_ref, hnext_ref, cnext_ref, cw_ref,
         wo_ref, g_ref, w1_ref, w2_ref, o_ref, h0_ref, h1_ref, acc0_ref, acc1_ref) = refs
    h_refs, acc_refs = (h0_ref, h1_ref), (acc0_ref, acc1_ref)
    i, j = pl.program_id(0), pl.program_id(1)
    rb = x_ref.shape[0]
    rows = pl.ds(pl.multiple_of(j * rb, rb), rb)

    def conv_block():
        pos = ((jnp.minimum(i, n_tiles - 1) * pl.num_programs(1) + j) * rb) % seq
        z = cg_ref[...] * hin_ref[...]
        z_before = (cprev_ref[...] * hprev_ref[...])[CONV_HALO - 1:CONV_HALO, :]
        z_after = (cnext_ref[...] * hnext_ref[...])[0:1, :]
        z_before = jnp.where(pos == 0, 0.0, z_before)
        z_after = jnp.where(pos + rb == seq, 0.0, z_after)
        r = lax.broadcasted_iota(jnp.int32, z.shape, 0)
        prev = jnp.where(r == 0, z_before, pltpu.roll(z, 1, 0))
        nxt = jnp.where(r == rb - 1, z_after, pltpu.roll(z, rb - 1, 0))
        y = cw_ref[0:1, :] * prev + cw_ref[1:2, :] * z + cw_ref[2:3, :] * nxt
        return bg_ref[...] * y

    def prepare(slot):
        if even:
            a = _sgu_block(u_ref[...], v_ref[...], sg_ref[...], ws_ref, bs_ref)
            b = _head_norm_gate(b_ref[...], gate_ref[...], ng_ref[...])
        else:
            a, b = conv_block(), b_ref[...]
        x1 = x_ref[...] + _dot(a.astype(BF16), wo_ref[:MIX_W, :])
        x1 += _dot(b.astype(BF16), wo_ref[MIX_W:, :])
        acc_refs[slot][rows, :] = x1
        h_refs[slot][rows, :] = _rms_rows(x1, g_ref[...]).astype(BF16)

    def mlp_chunk(slot):
        hm = _dot(h_refs[slot][...], w1_ref[...])
        act = jnp.square(jnp.maximum(hm, 0.0)).astype(BF16)
        acc_refs[slot][...] += _dot(act, w2_ref[...])

    odd_step = (i % 2) == 1

    @pl.when(i == 0)
    def _():
        prepare(0)

    @pl.when((i > 0) & (i < n_tiles) & jnp.logical_not(odd_step))
    def _():
        mlp_chunk(1)
        prepare(0)

    @pl.when(odd_step)
    def _():
        mlp_chunk(0)
        prepare(1)

    @pl.when(i == n_tiles)
    def _():
        mlp_chunk(1)

    last = j == pl.num_programs(1) - 1

    @pl.when(last & odd_step)
    def _():
        o_ref[...] = acc0_ref[...]

    @pl.when(last & (i > 0) & jnp.logical_not(odd_step))
    def _():
        o_ref[...] = acc1_ref[...]


def mix_mlp(x2, mix_in, wo_bf, g, w1_bf, w2_bf, seq, sgu_params=None, conv_w=None, tm=1024, tf=512):
    t, d = x2.shape
    ff = w1_bf.shape[1]
    n_tiles, n_j = t // tm, ff // tf
    rb = tm // n_j
    assert n_tiles % 2 == 0 and rb == SGU_CHUNK and seq % rb == 0
    even = sgu_params is not None

    def row_block(i, j):
        return jnp.minimum(i, n_tiles - 1) * n_j + j

    def blk(width):
        return pl.BlockSpec((rb, width), lambda i, j: (row_block(i, j), 0))

    def const(shape):
        return pl.BlockSpec(shape, lambda i, j: (0,) * len(shape))

    ins = [x2] + list(mix_in)
    specs = [blk(d)] + [blk(MIX_W) for _ in mix_in]
    if even:
        norm_g, w_s_bf, b_full, hgrn_norm_g = sgu_params
        ins += [norm_g.reshape(1, MIX_W), w_s_bf, b_full, hgrn_norm_g.reshape(1, MIX_W)]
        specs += [const((1, MIX_W)), const((N_HEADS, SGU_CHUNK, SGU_CHUNK)),
                  const((N_HEADS, SGU_CHUNK, HEAD_W)), const((1, MIX_W))]
    else:
        h_in, _, c_gate, _ = mix_in
        per = rb // CONV_HALO
        before = pl.BlockSpec((CONV_HALO, MIX_W),
                              lambda i, j: (jnp.maximum(row_block(i, j) * per - 1, 0), 0))
        after = pl.BlockSpec((CONV_HALO, MIX_W),
                             lambda i, j: (jnp.minimum((row_block(i, j) + 1) * per, t // CONV_HALO - 1), 0))
        ins += [h_in, c_gate, h_in, c_gate, conv_w]
        specs += [before, before, after, after, const((3, MIX_W))]
    return pl.pallas_call(
        functools.partial(_mix_mlp_kernel, even=even, n_tiles=n_tiles, seq=seq),
        grid=(n_tiles + 1, n_j),
        in_specs=specs + [
            const((2 * MIX_W, d)),
            const((1, d)),
            pl.BlockSpec((d, tf), lambda i, j: (0, j)),
            pl.BlockSpec((tf, d), lambda i, j: (j, 0)),
        ],
        out_specs=pl.BlockSpec((tm, d), lambda i, j: (jnp.maximum(i - 1, 0), 0)),
        out_shape=jax.ShapeDtypeStruct((t, d), F32),
        scratch_shapes=[pltpu.VMEM((tm, d), BF16), pltpu.VMEM((tm, d), BF16),
                        pltpu.VMEM((tm, d), F32), pltpu.VMEM((tm, d), F32)],
        compiler_params=_cparams(("arbitrary", "arbitrary")),
        name="mix_mlp",
    )(*ins, wo_bf, g.reshape(1, d), w1_bf, w2_bf)


SUBLANES = 8


def _hgrn_tables(c):
    levels = int(math.log2(c))
    assert 1 << levels == c
    pos = np.arange(c)
    t = pos[:, None]
    i = pos[None, :]
    d_rows, m_rows = [], []
    for lv in range(levels):
        h = c >> (lv + 1)
        start = (t // (2 * h)) * (2 * h)
        mid = start + h - 1
        upper = (t - start) >= h
        if h < SUBLANES:
            d_rows.append(np.where(upper, (i > mid) & (i <= t), (i > t) & (i <= mid)))
        same = (t // (2 * h)) == (i // (2 * h))
        m_rows.append(same & upper & ((i % (2 * h)) < h))
    n_fine = len(d_rows)
    d_rows.append(i <= t)
    d_rows.append(i > t)
    d = np.concatenate(d_rows, axis=0).astype(np.float32)
    m = np.concatenate(m_rows, axis=0).astype(np.float32)
    return d, m, levels, n_fine


def _flip_blocks(a, c):
    blocks = a.reshape(-1, c, c)
    return blocks[:, ::-1, ::-1].reshape(a.shape)


def _coarse_level(qh, kh, inc, h, fwd):
    c = inc.shape[0]
    zeros = jnp.zeros((h, inc.shape[1]), F32)
    q_rows, k_rows = [], []
    for n in range(c // (2 * h)):
        lo = slice(n * 2 * h, n * 2 * h + h)
        up = slice(n * 2 * h + h, (n + 1) * 2 * h)
        if fwd:
            ref = inc[n * 2 * h + h - 1:n * 2 * h + h, :]
            q_rows += [zeros, qh[up] * jnp.exp2(inc[up] - ref)]
            k_rows += [kh[lo] * jnp.exp2(ref - inc[lo]), zeros]
        else:
            ref = inc[n * 2 * h + h:n * 2 * h + h + 1, :]
            q_rows += [qh[lo] * jnp.exp2(inc[lo] - ref), zeros]
            k_rows += [zeros, kh[up] * jnp.exp2(ref - inc[up])]
    return jnp.concatenate(q_rows, axis=0), jnp.concatenate(k_rows, axis=0)


def _hgrn_direction(q, v_bf, vt_bf, a, lb, d_ref, m_ref, st_ref, *, levels, n_fine, fwd):
    c = q.shape[0]
    f = lb + (1.0 - lb) * (1.0 / (1.0 + jnp.exp(-a)))
    lf = jnp.log(f) * LOG2E
    kk = 1.0 - f
    hi, lo = _split2(lf)
    sums = _dot(d_ref[...], hi) + _dot(d_ref[...], lo)
    e_all = jnp.exp2(sums)
    inc_rows = slice(n_fine * c, (n_fine + 1) * c)
    exc_rows = slice((n_fine + 1) * c, (n_fine + 2) * c)
    last_row = c - 1 if fwd else 0
    row = lax.broadcasted_iota(jnp.int32, (c, c), 0)
    col = lax.broadcasted_iota(jnp.int32, (c, c), 1)
    eye = row == col
    outs = []
    for h in range(N_HEADS):
        cols = slice(h * HEAD_W, (h + 1) * HEAD_W)
        qh, kh, inc = q[:, cols], kk[:, cols], sums[inc_rows, cols]
        att = jnp.where(eye, jnp.sum(qh * kh, axis=-1, keepdims=True), 0.0)
        for lv in range(levels):
            half = c >> (lv + 1)
            if half >= SUBLANES:
                ql, kl = _coarse_level(qh, kh, inc, half, fwd)
            else:
                fine = lv - (levels - n_fine)
                e_lv = e_all[fine * c:(fine + 1) * c, cols]
                ql, kl = qh * e_lv, kh * e_lv
            pair = _dot_nt(ql.astype(BF16), kl.astype(BF16))
            att += pair if lv == 0 else pair * m_ref[lv * c:(lv + 1) * c, :]
        e_inc = e_all[inc_rows, cols]
        st = st_ref[h]
        o_h = _dot(att.astype(BF16), v_bf[:, cols])
        o_h += _dot_nt((qh * e_inc).astype(BF16), st.astype(BF16))
        k_dec = (kh * e_all[exc_rows, cols]).astype(BF16)
        st_ref[h] = st * e_inc[last_row:last_row + 1, :] + _dot(vt_bf[cols, :], k_dec)
        outs.append(o_h)
    return jnp.concatenate(outs, axis=-1)


HGRN_STEP_CHUNKS = 2


def _hgrn_kernel(qf_ref, qb_ref, if_ref, ib_ref, af_ref, ab_ref, lbf_ref, lbb_ref,
                 df_ref, db_ref, mf_ref, mb_ref, o_ref, stf_ref, stb_ref, *, levels, n_fine, chunk):
    n = pl.program_id(1)
    n_steps = pl.num_programs(1)
    blk = HGRN_STEP_CHUNKS * chunk

    @pl.when(n == 0)
    def _():
        stf_ref[...] = jnp.zeros_like(stf_ref)
        stb_ref[...] = jnp.zeros_like(stb_ref)

    def run(q_ref, i_ref, a_ref, lb_ref, d_ref, m_ref, st_ref, sub, fwd):
        rows = slice(sub * chunk, (sub + 1) * chunk)
        v = i_ref[rows, :]
        return _hgrn_direction(q_ref[rows, :], v.astype(BF16), v.T.astype(BF16), a_ref[rows, :],
                               lb_ref[...], d_ref, m_ref, st_ref, levels=levels, n_fine=n_fine, fwd=fwd)

    o_f, o_b = [], [None] * HGRN_STEP_CHUNKS
    for sub in range(HGRN_STEP_CHUNKS):
        rev = HGRN_STEP_CHUNKS - 1 - sub
        o_f.append(run(qf_ref, if_ref, af_ref, lbf_ref, df_ref, mf_ref, stf_ref, sub, True))
        o_b[rev] = run(qb_ref, ib_ref, ab_ref, lbb_ref, db_ref, mb_ref, stb_ref, rev, False)
    o_f = jnp.concatenate(o_f, axis=0)
    o_b = jnp.concatenate(o_b, axis=0)
    rows_f = pl.ds(pl.multiple_of(n * blk, blk), blk)
    rows_b = pl.ds(pl.multiple_of((n_steps - 1 - n) * blk, blk), blk)

    @pl.when(2 * n < n_steps)
    def _():
        o_ref[rows_f, :] = o_f
        o_ref[rows_b, :] = o_b

    @pl.when(2 * n >= n_steps)
    def _():
        o_ref[rows_f, :] += o_f
        o_ref[rows_b, :] += o_b


def hgrn2(q, i, a_fwd, a_bwd, lb, bsz, seq, chunk=HGRN_CHUNK):
    blk = HGRN_STEP_CHUNKS * chunk
    n_steps = seq // blk
    assert n_steps % 2 == 0
    d_np, m_np, levels, n_fine = _hgrn_tables(chunk)
    d_f = jnp.asarray(d_np, BF16)
    d_b = jnp.asarray(_flip_blocks(d_np, chunk), BF16)
    m_f = jnp.asarray(m_np, F32)
    m_b = jnp.asarray(_flip_blocks(m_np, chunk), F32)

    fwd = pl.BlockSpec((blk, MIX_W), lambda b, n: (b * n_steps + n, 0))
    bwd = pl.BlockSpec((blk, MIX_W), lambda b, n: (b * n_steps + n_steps - 1 - n, 0))
    vec = pl.BlockSpec((1, MIX_W), lambda b, n: (0, 0))

    def table(a):
        return pl.BlockSpec(a.shape, lambda b, n: (0, 0))

    return pl.pallas_call(
        functools.partial(_hgrn_kernel, levels=levels, n_fine=n_fine, chunk=chunk),
        grid=(bsz, n_steps),
        in_specs=[fwd, bwd, fwd, bwd, fwd, bwd, vec, vec,
                  table(d_f), table(d_b), table(m_f), table(m_b)],
        out_specs=pl.BlockSpec((seq, MIX_W), lambda b, n: (b, 0)),
        out_shape=jax.ShapeDtypeStruct((bsz * seq, MIX_W), F32),
        scratch_shapes=[pltpu.VMEM((N_HEADS, HEAD_W, HEAD_W), F32),
                        pltpu.VMEM((N_HEADS, HEAD_W, HEAD_W), F32)],
        compiler_params=_cparams(("parallel", "arbitrary")),
        name="hgrn2",
    )(q, q, i, i, a_fwd, a_bwd, lb[0:1], lb[1:2], d_f, d_b, m_f, m_b)


ATTN_TQ = 1024
ATTN_TW = 512


def _qk_prep(x, gain, cos, sin_signed, seg_mean):
    hi, lo = _split2(x * x)
    ms = _dot(hi, seg_mean) + _dot(lo, seg_mean)
    y = x * lax.rsqrt(ms + EPS) * gain
    lane = lax.broadcasted_iota(jnp.int32, y.shape, 1)
    half = QK_DIM // 2
    partner = jnp.where((lane % QK_DIM) < half,
                        pltpu.roll(y, HEAD_W - half, 1), pltpu.roll(y, half, 1))
    return y * cos + partner * sin_signed


ATTN_ONES_ROWS = 16
ATTN_KB = 512


def _attn_kernel(q_ref, k_ref, v_ref, cosq_ref, sinq_ref, cosk_ref, sink_ref, qg_ref, kg_ref,
                 sg_ref, lam_ref, mean_ref, o_ref, kh_ref, v1t_ref, qp_ref, s0_ref, s1_ref, pa_ref, pb_ref,
                 acc_ref, *, seq, out_scale):
    qi = pl.program_id(2)
    seg_mean = mean_ref[...]
    n_kb = seq // ATTN_KB

    @pl.when(qi == 0)
    def _():
        def prep(r, carry):
            rows = pl.ds(pl.multiple_of(r * ATTN_KB, ATTN_KB), ATTN_KB)
            kh_ref[r] = _qk_prep(k_ref[rows, :], kg_ref[...], cosk_ref[rows, :],
                                 sink_ref[rows, :], seg_mean).astype(BF16)
            v1t_ref[r, 0:HEAD_W, :] = v_ref[rows, :].T.astype(BF16)
            v1t_ref[r, HEAD_W:HEAD_W + ATTN_ONES_ROWS, :] = jnp.ones((ATTN_ONES_ROWS, ATTN_KB), BF16)
            return carry

        lax.fori_loop(0, n_kb, prep, 0)

    tw = ATTN_TW
    n_units = 2 * (q_ref.shape[0] // tw)
    scale = (QK_DIM ** -0.5) * LOG2E
    for t in range(n_units // 2):
        rows = slice(t * tw, (t + 1) * tw)
        qh = _qk_prep(q_ref[rows, :], qg_ref[...], cosq_ref[rows, :], sinq_ref[rows, :], seg_mean) * scale
        lane = lax.broadcasted_iota(jnp.int32, qh.shape, 1)
        qp_ref[2 * t] = jnp.where(lane < QK_DIM, qh, 0.0).astype(BF16)
        qp_ref[2 * t + 1] = jnp.where(lane >= QK_DIM, qh, 0.0).astype(BF16)

    s_refs, p_refs = (s0_ref, s1_ref), (pa_ref, pb_ref)
    neg = jnp.full((8, tw), -jnp.inf, F32)

    def phase(us, ue, m_e):
        def sc(kb, m8):
            if us is None:
                return m8
            s_t = _dot_nt(kh_ref[kb], qp_ref[us])
            s_refs[us % 2][kb] = s_t
            return jnp.maximum(m8, jnp.max(s_t.reshape(ATTN_KB // 8, 8, tw), axis=0))

        def pr(kb, slot):
            if ue is not None:
                p_refs[slot][...] = jnp.exp2(s_refs[ue % 2][kb] - m_e).astype(BF16)

        def va(kb, slot, first=False):
            if ue is not None:
                part = _dot(v1t_ref[kb], p_refs[slot][...])
                acc_ref[ue] = part if first else acc_ref[ue] + part

        m8 = neg
        for kb in range(n_kb):
            m8 = sc(kb, m8)
            pr(kb, kb % 2)
            if kb > 0:
                va(kb - 1, (kb - 1) % 2, first=(kb == 1))
        va(n_kb - 1, (n_kb - 1) % 2)
        return jnp.max(m8, axis=0, keepdims=True)

    m = phase(0, None, None)
    for u in range(1, n_units):
        m = phase(u, u - 1, m)
    phase(None, n_units - 1, m)

    for t in range(n_units // 2):
        parts = [acc_ref[2 * t + c, 0:HEAD_W, :] / acc_ref[2 * t + c, HEAD_W:HEAD_W + 1, :]
                 for c in range(2)]
        o = (parts[0] - lam_ref[0, 0] * parts[1]).T
        o_ref[t * tw:(t + 1) * tw, :] = _rms_rows(o, sg_ref[...]) * out_scale


def diff_attention(q, k, v, cos_t, sin_t, qg, kg, sub_g, lam, lambda_init, bsz, seq, tq=ATTN_TQ):
    nq = seq // tq
    n_kb = seq // ATTN_KB
    assert n_kb >= 4 and n_kb % 2 == 0
    seg = np.kron(np.eye(2, dtype=np.float32), np.full((QK_DIM, QK_DIM), 1.0 / QK_DIM, np.float32))
    qblk = pl.BlockSpec((tq, HEAD_W), lambda b, h, i: (b * nq + i, h))
    kblk = pl.BlockSpec((seq, HEAD_W), lambda b, h, i: (b, h))
    tabq = pl.BlockSpec((tq, HEAD_W), lambda b, h, i: (i, 0))
    tabk = pl.BlockSpec((seq, HEAD_W), lambda b, h, i: (0, 0))
    vec = pl.BlockSpec((1, HEAD_W), lambda b, h, i: (0, 0))
    return pl.pallas_call(
        functools.partial(_attn_kernel, seq=seq, out_scale=1.0 - lambda_init),
        grid=(bsz, N_HEADS, nq),
        in_specs=[qblk, kblk, kblk, tabq, tabq, tabk, tabk, vec, vec, vec,
                  pl.BlockSpec(memory_space=pltpu.SMEM),
                  pl.BlockSpec((HEAD_W, HEAD_W), lambda b, h, i: (0, 0))],
        out_specs=qblk,
        out_shape=jax.ShapeDtypeStruct((bsz * seq, MIX_W), F32),
        scratch_shapes=[pltpu.VMEM((n_kb, ATTN_KB, HEAD_W), BF16),
                        pltpu.VMEM((n_kb, HEAD_W + ATTN_ONES_ROWS, ATTN_KB), BF16),
                        pltpu.VMEM((2 * (tq // ATTN_TW), ATTN_TW, HEAD_W), BF16),
                        pltpu.VMEM((n_kb, ATTN_KB, ATTN_TW), F32),
                        pltpu.VMEM((n_kb, ATTN_KB, ATTN_TW), F32),
                        pltpu.VMEM((ATTN_KB, ATTN_TW), BF16),
                        pltpu.VMEM((ATTN_KB, ATTN_TW), BF16),
                        pltpu.VMEM((2 * (tq // ATTN_TW), HEAD_W + ATTN_ONES_ROWS, ATTN_TW), F32)],
        compiler_params=_cparams(("parallel", "parallel", "arbitrary")),
        name="diff_attention",
    )(q, k, v, cos_t, sin_t, cos_t, sin_t,
      jnp.tile(qg, 2).reshape(1, HEAD_W), jnp.tile(kg, 2).reshape(1, HEAD_W),
      sub_g.reshape(1, HEAD_W), jnp.reshape(lam, (1, 1)).astype(F32), jnp.asarray(seg, BF16))


def _rope_tables(seq):
    inv = 1.0 / (ROPE_THETA ** (jnp.arange(0, QK_DIM, 2, dtype=F32) / QK_DIM))
    ang = jnp.arange(seq, dtype=F32)[:, None] * inv[None, :]
    cos, sin = jnp.cos(ang), jnp.sin(ang)
    return jnp.tile(cos, (1, 4)), jnp.tile(jnp.concatenate([-sin, sin], axis=-1), (1, 2))


def kernel(x, norm_mix_g, norm_mlp_g, w_in_even, w_out_even, sgu_norm_g, sgu_w, sgu_b,
           hgrn_lb_logits, hgrn_norm_g, w_in_odd, w_out_odd, conv_w, q_norm_g, k_norm_g,
           lambda_q1, lambda_k1, lambda_q2, lambda_k2, diff_norm_g, mlp_w1, mlp_w2):
    bsz, seq, d = x.shape
    depth = norm_mix_g.shape[0]
    cos_t, sin_t = _rope_tables(seq)
    p_lb = jax.nn.softmax(hgrn_lb_logits.astype(F32), axis=1)
    lower_bounds = jnp.cumsum(p_lb, axis=1) - p_lb[:, :1]

    x2 = x.reshape(bsz * seq, d)
    for l in range(depth):
        if l % 2 == 0:
            e = l // 2
            u, v, q, i, g, a_fwd, a_bwd = norm_proj(x2, norm_mix_g[l], w_in_even, e)
            b_full = jnp.broadcast_to(sgu_b[e][:, :, None], (N_HEADS, SGU_CHUNK, HEAD_W))
            o_sum = hgrn2(q, i, a_fwd, a_bwd, lower_bounds[:, e], bsz, seq)
            x2 = mix_mlp(x2, (u, v, o_sum, g), w_out_even[e].astype(BF16), norm_mlp_g[l],
                         mlp_w1[l].astype(BF16), mlp_w2[l].astype(BF16), seq,
                         sgu_params=(sgu_norm_g[e], sgu_w[e].astype(BF16), b_full, hgrn_norm_g[e]))
        else:
            o = l // 2
            h_in, b_gate, c_gate, q, k, v = norm_proj(x2, norm_mix_g[l], w_in_odd, o)
            lambda_init = 0.8 - 0.6 * math.exp(-0.3 * l)
            lam = (jnp.exp(jnp.sum(lambda_q1[o] * lambda_k1[o]))
                   - jnp.exp(jnp.sum(lambda_q2[o] * lambda_k2[o])) + lambda_init)
            out_d = diff_attention(q, k, v, cos_t, sin_t, q_norm_g[o], k_norm_g[o], diff_norm_g[o],
                                   lam, lambda_init, bsz, seq)
            x2 = mix_mlp(x2, (h_in, b_gate, c_gate, out_d), w_out_odd[o].astype(BF16), norm_mlp_g[l],
                         mlp_w1[l].astype(BF16), mlp_w2[l].astype(BF16), seq, conv_w=conv_w[o])
    return x2.reshape(bsz, seq, d)
```

```python
import functools
import math

import numpy as np
import jax
import jax.numpy as jnp
from jax import lax
from jax.experimental import pallas as pl
from jax.experimental.pallas import tpu as pltpu

F32 = jnp.float32
BF16 = jnp.bfloat16

EPS = 1e-6
ROPE_THETA = 10000.0
LOG2E = 1.4426950408889634

HEAD_W = 128
N_HEADS = 4
MIX_W = HEAD_W * N_HEADS
SGU_CHUNK = 128
HGRN_CHUNK = 128
QK_DIM = 64

VMEM_LIMIT = 56 * 1024 * 1024


def _cparams(sem):
    return pltpu.CompilerParams(dimension_semantics=sem, vmem_limit_bytes=VMEM_LIMIT)


def _dot(a, b):
    return jnp.dot(a, b, preferred_element_type=F32)


def _dot_nt(a, b):
    return lax.dot_general(a, b, (((1,), (1,)), ((), ())), preferred_element_type=F32)


def _split2(x):
    hi = x.astype(BF16)
    lo = (x - hi.astype(F32)).astype(BF16)
    return hi, lo


def _rms_rows(x, g):
    ms = jnp.mean(x * x, axis=-1, keepdims=True)
    return x * lax.rsqrt(ms + EPS) * g


def _norm_proj_kernel(x_ref, g_ref, w_ref, *rest):
    o_refs, wbf_ref = rest[:-1], rest[-1]

    @pl.when(pl.program_id(0) == 0)
    def _():
        for s in range(len(o_refs)):
            cols = slice(s * MIX_W, (s + 1) * MIX_W)
            wbf_ref[:, cols] = w_ref[:, cols].astype(BF16)

    h = _rms_rows(x_ref[...], g_ref[...]).astype(BF16)
    for s, o_ref in enumerate(o_refs):
        o_ref[...] = _dot(h, wbf_ref[:, s * MIX_W:(s + 1) * MIX_W])


def norm_proj(x2, g, w_stack, layer, tm=512):
    t, d = x2.shape
    n = w_stack.shape[2]
    n_split = n // MIX_W
    return pl.pallas_call(
        _norm_proj_kernel,
        grid=(t // tm,),
        in_specs=[
            pl.BlockSpec((tm, d), lambda i: (i, 0)),
            pl.BlockSpec((1, d), lambda i: (0, 0)),
            pl.BlockSpec((None, d, n), lambda i: (layer, 0, 0), pipeline_mode=pl.Buffered(1)),
        ],
        out_specs=[pl.BlockSpec((tm, MIX_W), lambda i: (i, 0)) for _ in range(n_split)],
        out_shape=[jax.ShapeDtypeStruct((t, MIX_W), F32) for _ in range(n_split)],
        scratch_shapes=[pltpu.VMEM((d, n), BF16)],
        compiler_params=_cparams(("arbitrary",)),
        name="norm_proj",
    )(x2, g.reshape(1, d), w_stack)


def _head_norm_gate(o, g, ng):
    outs = []
    for h in range(N_HEADS):
        cols = slice(h * HEAD_W, (h + 1) * HEAD_W)
        gh = g[:, cols]
        outs.append(_rms_rows(o[:, cols], ng[:, cols]) * (gh * (1.0 / (1.0 + jnp.exp(-gh)))))
    return jnp.concatenate(outs, axis=-1)


def _sgu_block(u, v, g, w_ref, b_ref):
    outs = []
    for h in range(N_HEADS):
        cols = slice(h * HEAD_W, (h + 1) * HEAD_W)
        vn = _rms_rows(jax.nn.gelu(v[:, cols]), g[:, cols])
        mixed = _dot(w_ref[h], vn.astype(BF16)) + b_ref[h]
        outs.append(jax.nn.gelu(u[:, cols]) * mixed)
    return jnp.concatenate(outs, axis=-1)


CONV_HALO = 8


def _mix_mlp_kernel(*refs, even, n_tiles, seq):
    if even:
        (x_ref, u_ref, v_ref, b_ref, gate_ref, sg_ref, ws_ref, bs_ref, ng_ref,
         wo_ref, g_ref, w1_ref, w2_ref, o_ref, h0_ref, h1_ref, acc0_ref, acc1_ref) = refs
    else:
        (x_ref, hin_ref, bg_ref, cg_ref, b_ref, hprev_ref, cprev_ref, hnext_ref, cnext_ref, cw_ref,
         wo_ref, g_ref, w1_ref, w2_ref, o_ref, h0_ref, h1_ref, acc0_ref, acc1_ref) = refs
    h_refs, acc_refs = (h0_ref, h1_ref), (acc0_ref, acc1_ref)
    i, j = pl.program_id(0), pl.program_id(1)
    rb = x_ref.shape[0]
    rows = pl.ds(pl.multiple_of(j * rb, rb), rb)

    def conv_block():
        pos = ((jnp.minimum(i, n_tiles - 1) * pl.num_programs(1) + j) * rb) % seq
        z = cg_ref[...] * hin_ref[...]
        z_before = (cprev_ref[...] * hprev_ref[...])[CONV_HALO - 1:CONV_HALO, :]
        z_after = (cnext_ref[...] * hnext_ref[...])[0:1, :]
        z_before = jnp.where(pos == 0, 0.0, z_before)
        z_after = jnp.where(pos + rb == seq, 0.0, z_after)
        r = lax.broadcasted_iota(jnp.int32, z.shape, 0)
        prev = jnp.where(r == 0, z_before, pltpu.roll(z, 1, 0))
        nxt = jnp.where(r == rb - 1, z_after, pltpu.roll(z, rb - 1, 0))
        y = cw_ref[0:1, :] * prev + cw_ref[1:2, :] * z + cw_ref[2:3, :] * nxt
        return bg_ref[...] * y

    def prepare(slot):
        if even:
            a = _sgu_block(u_ref[...], v_ref[...], sg_ref[...], ws_ref, bs_ref)
            b = _head_norm_gate(b_ref[...], gate_ref[...], ng_ref[...])
        else:
            a, b = conv_block(), b_ref[...]
        x1 = x_ref[...] + _dot(a.astype(BF16), wo_ref[:MIX_W, :])
        x1 += _dot(b.astype(BF16), wo_ref[MIX_W:, :])
        acc_refs[slot][rows, :] = x1
        h_refs[slot][rows, :] = _rms_rows(x1, g_ref[...]).astype(BF16)

    def mlp_chunk(slot):
        hm = _dot(h_refs[slot][...], w1_ref[...])
        act = jnp.square(jnp.maximum(hm, 0.0)).astype(BF16)
        acc_refs[slot][...] += _dot(act, w2_ref[...])

    odd_step = (i % 2) == 1

    @pl.when(i == 0)
    def _():
        prepare(0)

    @pl.when((i > 0) & (i < n_tiles) & jnp.logical_not(odd_step))
    def _():
        mlp_chunk(1)
        prepare(0)

    @pl.when(odd_step)
    def _():
        mlp_chunk(0)
        prepare(1)

    @pl.when(i == n_tiles)
    def _():
        mlp_chunk(1)

    last = j == pl.num_programs(1) - 1

    @pl.when(last & odd_step)
    def _():
        o_ref[...] = acc0_ref[...]

    @pl.when(last & (i > 0) & jnp.logical_not(odd_step))
    def _():
        o_ref[...] = acc1_ref[...]


def mix_mlp(x2, mix_in, wo_bf, wo_layer, g, w1_bf, w2_bf, layer, seq, sgu_params=None, conv_w=None,
            tm=1024, tf=512):
    t, d = x2.shape
    ff = w1_bf.shape[2]
    n_tiles, n_j = t // tm, ff // tf
    rb = tm // n_j
    assert n_tiles % 2 == 0 and rb == SGU_CHUNK and seq % rb == 0
    even = sgu_params is not None

    def row_block(i, j):
        return jnp.minimum(i, n_tiles - 1) * n_j + j

    def blk(width):
        return pl.BlockSpec((rb, width), lambda i, j: (row_block(i, j), 0))

    def const(shape):
        return pl.BlockSpec(shape, lambda i, j: (0,) * len(shape))

    ins = [x2] + list(mix_in)
    specs = [blk(d)] + [blk(MIX_W) for _ in mix_in]
    if even:
        norm_g, w_s_bf, b_full, hgrn_norm_g = sgu_params
        ins += [norm_g.reshape(1, MIX_W), w_s_bf, b_full, hgrn_norm_g.reshape(1, MIX_W)]
        specs += [const((1, MIX_W)), const((N_HEADS, SGU_CHUNK, SGU_CHUNK)),
                  const((N_HEADS, SGU_CHUNK, HEAD_W)), const((1, MIX_W))]
    else:
        h_in, _, c_gate, _ = mix_in
        per = rb // CONV_HALO
        before = pl.BlockSpec((CONV_HALO, MIX_W),
                              lambda i, j: (jnp.maximum(row_block(i, j) * per - 1, 0), 0))
        after = pl.BlockSpec((CONV_HALO, MIX_W),
                             lambda i, j: (jnp.minimum((row_block(i, j) + 1) * per, t // CONV_HALO - 1), 0))
        ins += [h_in, c_gate, h_in, c_gate, conv_w]
        specs += [before, before, after, after, const((3, MIX_W))]
    return pl.pallas_call(
        functools.partial(_mix_mlp_kernel, even=even, n_tiles=n_tiles, seq=seq),
        grid=(n_tiles + 1, n_j),
        in_specs=specs + [
            pl.BlockSpec((None, 2 * MIX_W, d), lambda i, j: (wo_layer, 0, 0)),
            const((1, d)),
            pl.BlockSpec((None, d, tf), lambda i, j: (layer, 0, j)),
            pl.BlockSpec((None, tf, d), lambda i, j: (layer, j, 0)),
        ],
        out_specs=pl.BlockSpec((tm, d), lambda i, j: (jnp.maximum(i - 1, 0), 0)),
        out_shape=jax.ShapeDtypeStruct((t, d), F32),
        scratch_shapes=[pltpu.VMEM((tm, d), BF16), pltpu.VMEM((tm, d), BF16),
                        pltpu.VMEM((tm, d), F32), pltpu.VMEM((tm, d), F32)],
        compiler_params=_cparams(("arbitrary", "arbitrary")),
        name="mix_mlp",
    )(*ins, wo_bf, g.reshape(1, d), w1_bf, w2_bf)


SUBLANES = 8


def _hgrn_tables(c):
    levels = int(math.log2(c))
    assert 1 << levels == c
    pos = np.arange(c)
    t = pos[:, None]
    i = pos[None, :]
    d_rows, m_rows = [], []
    for lv in range(levels):
        h = c >> (lv + 1)
        start = (t // (2 * h)) * (2 * h)
        mid = start + h - 1
        upper = (t - start) >= h
        if h < SUBLANES:
            d_rows.append(np.where(upper, (i > mid) & (i <= t), (i > t) & (i <= mid)))
        same = (t // (2 * h)) == (i // (2 * h))
        m_rows.append(same & upper & ((i % (2 * h)) < h))
    n_fine = len(d_rows)
    d_rows.append(i <= t)
    d = np.concatenate(d_rows, axis=0).astype(np.float32)
    m = np.concatenate(m_rows, axis=0).astype(np.float32)
    return d, m, levels, n_fine


def _flip_blocks(a, c):
    blocks = a.reshape(-1, c, c)
    return blocks[:, ::-1, ::-1].reshape(a.shape)


def _coarse_level(qh, kh, inc, h, fwd):
    c = inc.shape[0]
    zeros = jnp.zeros((h, inc.shape[1]), F32)
    q_rows, k_rows = [], []
    for n in range(c // (2 * h)):
        lo = slice(n * 2 * h, n * 2 * h + h)
        up = slice(n * 2 * h + h, (n + 1) * 2 * h)
        if fwd:
            ref = inc[n * 2 * h + h - 1:n * 2 * h + h, :]
            q_rows += [zeros, qh[up] * jnp.exp2(inc[up] - ref)]
            k_rows += [kh[lo] * jnp.exp2(ref - inc[lo]), zeros]
        else:
            ref = inc[n * 2 * h + h:n * 2 * h + h + 1, :]
            q_rows += [qh[lo] * jnp.exp2(inc[lo] - ref), zeros]
            k_rows += [zeros, kh[up] * jnp.exp2(ref - inc[up])]
    return jnp.concatenate(q_rows, axis=0), jnp.concatenate(k_rows, axis=0)


def _hgrn_direction(q, v_bf, vt_bf, a, lb, d_ref, m_ref, st_ref, *, levels, n_fine, fwd):
    c = q.shape[0]
    f = lb + (1.0 - lb) * (1.0 / (1.0 + jnp.exp(-a)))
    lf = jnp.log(f) * LOG2E
    kk = 1.0 - f
    hi, lo = _split2(lf)
    sums = _dot(d_ref[...], hi) + _dot(d_ref[...], lo)
    e_all = jnp.exp2(sums)
    inc_rows = slice(n_fine * c, (n_fine + 1) * c)
    last_row = c - 1 if fwd else 0
    row = lax.broadcasted_iota(jnp.int32, (c, c), 0)
    col = lax.broadcasted_iota(jnp.int32, (c, c), 1)
    eye = row == col
    outs = []
    for h in range(N_HEADS):
        cols = slice(h * HEAD_W, (h + 1) * HEAD_W)
        qh, kh, inc = q[:, cols], kk[:, cols], sums[inc_rows, cols]
        att = jnp.where(eye, jnp.sum(qh * kh, axis=-1, keepdims=True), 0.0)
        for lv in range(levels):
            half = c >> (lv + 1)
            if half >= SUBLANES:
                ql, kl = _coarse_level(qh, kh, inc, half, fwd)
            else:
                fine = lv - (levels - n_fine)
                e_lv = e_all[fine * c:(fine + 1) * c, cols]
                ql, kl = qh * e_lv, kh * e_lv
            pair = _dot_nt(ql.astype(BF16), kl.astype(BF16))
            att += pair if lv == 0 else pair * m_ref[lv * c:(lv + 1) * c, :]
        e_inc = e_all[inc_rows, cols]
        st = st_ref[h]
        o_h = _dot(att.astype(BF16), v_bf[:, cols])
        o_h += _dot_nt((qh * e_inc).astype(BF16), st.astype(BF16))
        k_dec = (kh * jnp.exp2(inc[last_row:last_row + 1, :] - inc)).astype(BF16)
        st_ref[h] = st * e_inc[last_row:last_row + 1, :] + _dot(vt_bf[cols, :], k_dec)
        outs.append(o_h)
    return jnp.concatenate(outs, axis=-1)


HGRN_STEP_CHUNKS = 2


def _hgrn_kernel(qf_ref, qb_ref, if_ref, ib_ref, af_ref, ab_ref, lbf_ref, lbb_ref,
                 df_ref, db_ref, mf_ref, mb_ref, o_ref, stf_ref, stb_ref, *, levels, n_fine, chunk):
    n = pl.program_id(1)
    n_steps = pl.num_programs(1)
    blk = HGRN_STEP_CHUNKS * chunk

    @pl.when(n == 0)
    def _():
        stf_ref[...] = jnp.zeros_like(stf_ref)
        stb_ref[...] = jnp.zeros_like(stb_ref)

    def run(q_ref, i_ref, a_ref, lb_ref, d_ref, m_ref, st_ref, sub, fwd):
        rows = slice(sub * chunk, (sub + 1) * chunk)
        v = i_ref[rows, :]
        return _hgrn_direction(q_ref[rows, :], v.astype(BF16), v.T.astype(BF16), a_ref[rows, :],
                               lb_ref[...], d_ref, m_ref, st_ref, levels=levels, n_fine=n_fine, fwd=fwd)

    o_f, o_b = [], [None] * HGRN_STEP_CHUNKS
    for sub in range(HGRN_STEP_CHUNKS):
        rev = HGRN_STEP_CHUNKS - 1 - sub
        o_f.append(run(qf_ref, if_ref, af_ref, lbf_ref, df_ref, mf_ref, stf_ref, sub, True))
        o_b[rev] = run(qb_ref, ib_ref, ab_ref, lbb_ref, db_ref, mb_ref, stb_ref, rev, False)
    o_f = jnp.concatenate(o_f, axis=0)
    o_b = jnp.concatenate(o_b, axis=0)
    rows_f = pl.ds(pl.multiple_of(n * blk, blk), blk)
    rows_b = pl.ds(pl.multiple_of((n_steps - 1 - n) * blk, blk), blk)

    @pl.when(2 * n < n_steps)
    def _():
        o_ref[rows_f, :] = o_f
        o_ref[rows_b, :] = o_b

    @pl.when(2 * n >= n_steps)
    def _():
        o_ref[rows_f, :] += o_f
        o_ref[rows_b, :] += o_b


def hgrn2(q, i, a_fwd, a_bwd, lb, bsz, seq, chunk=HGRN_CHUNK):
    blk = HGRN_STEP_CHUNKS * chunk
    n_steps = seq // blk
    assert n_steps % 2 == 0
    d_np, m_np, levels, n_fine = _hgrn_tables(chunk)
    d_f = jnp.asarray(d_np, BF16)
    d_b = jnp.asarray(_flip_blocks(d_np, chunk), BF16)
    m_f = jnp.asarray(m_np, F32)
    m_b = jnp.asarray(_flip_blocks(m_np, chunk), F32)

    fwd = pl.BlockSpec((blk, MIX_W), lambda b, n: (b * n_steps + n, 0))
    bwd = pl.BlockSpec((blk, MIX_W), lambda b, n: (b * n_steps + n_steps - 1 - n, 0))
    vec = pl.BlockSpec((1, MIX_W), lambda b, n: (0, 0))

    def table(a):
        return pl.BlockSpec(a.shape, lambda b, n: (0, 0))

    return pl.pallas_call(
        functools.partial(_hgrn_kernel, levels=levels, n_fine=n_fine, chunk=chunk),
        grid=(bsz, n_steps),
        in_specs=[fwd, bwd, fwd, bwd, fwd, bwd, vec, vec,
                  table(d_f), table(d_b), table(m_f), table(m_b)],
        out_specs=pl.BlockSpec((seq, MIX_W), lambda b, n: (b, 0)),
        out_shape=jax.ShapeDtypeStruct((bsz * seq, MIX_W), F32),
        scratch_shapes=[pltpu.VMEM((N_HEADS, HEAD_W, HEAD_W), F32),
                        pltpu.VMEM((N_HEADS, HEAD_W, HEAD_W), F32)],
        compiler_params=_cparams(("parallel", "arbitrary")),
        name="hgrn2",
    )(q, q, i, i, a_fwd, a_bwd, lb[0:1], lb[1:2], d_f, d_b, m_f, m_b)


ATTN_TQ = 1024
ATTN_TW = 512


def _qk_prep(x, gain, cos, sin_signed, seg_mean):
    hi, lo = _split2(x * x)
    ms = _dot(hi, seg_mean) + _dot(lo, seg_mean)
    y = x * lax.rsqrt(ms + EPS) * gain
    lane = lax.broadcasted_iota(jnp.int32, y.shape, 1)
    half = QK_DIM // 2
    partner = jnp.where((lane % QK_DIM) < half,
                        pltpu.roll(y, HEAD_W - half, 1), pltpu.roll(y, half, 1))
    return y * cos + partner * sin_signed


ATTN_ONES_ROWS = 16
ATTN_KB = 2048


def _attn_kernel(q_ref, k_ref, v_ref, cosq_ref, sinq_ref, cosk_ref, sink_ref, qg_ref, kg_ref,
                 sg_ref, lam_ref, mean_ref, o_ref, kh_ref, v1t_ref, qp_ref, s0_ref, s1_ref, pa_ref, pb_ref,
                 acc_ref, *, seq, out_scale):
    qi = pl.program_id(2)
    seg_mean = mean_ref[...]
    n_kb = seq // ATTN_KB

    @pl.when(qi == 0)
    def _():
        def prep(r, carry):
            rows = pl.ds(pl.multiple_of(r * ATTN_KB, ATTN_KB), ATTN_KB)
            kh_ref[r] = _qk_prep(k_ref[rows, :], kg_ref[...], cosk_ref[rows, :],
                                 sink_ref[rows, :], seg_mean).astype(BF16)
            v1t_ref[r, 0:HEAD_W, :] = v_ref[rows, :].T.astype(BF16)
            v1t_ref[r, HEAD_W:HEAD_W + ATTN_ONES_ROWS, :] = jnp.ones((ATTN_ONES_ROWS, ATTN_KB), BF16)
            return carry

        lax.fori_loop(0, n_kb, prep, 0)

    tw = ATTN_TW
    n_units = 2 * (q_ref.shape[0] // tw)
    scale = (QK_DIM ** -0.5) * LOG2E
    for t in range(n_units // 2):
        rows = slice(t * tw, (t + 1) * tw)
        qh = _qk_prep(q_ref[rows, :], qg_ref[...], cosq_ref[rows, :], sinq_ref[rows, :], seg_mean) * scale
        lane = lax.broadcasted_iota(jnp.int32, qh.shape, 1)
        qp_ref[2 * t] = jnp.where(lane < QK_DIM, qh, 0.0).astype(BF16)
        qp_ref[2 * t + 1] = jnp.where(lane >= QK_DIM, qh, 0.0).astype(BF16)

    s_refs, p_refs = (s0_ref, s1_ref), (pa_ref, pb_ref)
    neg = jnp.full((8, tw), -jnp.inf, F32)

    def phase(us, ue, m_e):
        def sc(kb, m8):
            if us is None:
                return m8
            s_t = _dot_nt(kh_ref[kb], qp_ref[us])
            s_refs[us % 2][kb] = s_t
            return jnp.maximum(m8, jnp.max(s_t.reshape(ATTN_KB // 8, 8, tw), axis=0))

        def pr(kb, slot):
            if ue is not None:
                p_refs[slot][...] = jnp.exp2(s_refs[ue % 2][kb] - m_e).astype(BF16)

        def va(kb, slot, first=False):
            if ue is not None:
                part = _dot(v1t_ref[kb], p_refs[slot][...])
                acc_ref[ue] = part if first else acc_ref[ue] + part

        m8 = neg
        for kb in range(n_kb):
            m8 = sc(kb, m8)
            pr(kb, kb % 2)
            if kb > 0:
                va(kb - 1, (kb - 1) % 2, first=(kb == 1))
        va(n_kb - 1, (n_kb - 1) % 2, first=(n_kb == 1))
        return jnp.max(m8, axis=0, keepdims=True)

    m = phase(0, None, None)
    for u in range(1, n_units):
        m = phase(u, u - 1, m)
    phase(None, n_units - 1, m)

    for t in range(n_units // 2):
        parts = [acc_ref[2 * t + c, 0:HEAD_W, :] / acc_ref[2 * t + c, HEAD_W:HEAD_W + 1, :]
                 for c in range(2)]
        o = (parts[0] - lam_ref[0, 0] * parts[1]).T
        o_ref[t * tw:(t + 1) * tw, :] = _rms_rows(o, sg_ref[...]) * out_scale


def diff_attention(q, k, v, cos_t, sin_t, qg, kg, sub_g, lam, lambda_init, bsz, seq, tq=ATTN_TQ):
    nq = seq // tq
    n_kb = seq // ATTN_KB
    assert seq % ATTN_KB == 0
    seg = np.kron(np.eye(2, dtype=np.float32), np.full((QK_DIM, QK_DIM), 1.0 / QK_DIM, np.float32))
    qblk = pl.BlockSpec((tq, HEAD_W), lambda b, h, i: (b * nq + i, h))
    kblk = pl.BlockSpec((seq, HEAD_W), lambda b, h, i: (b, h))
    tabq = pl.BlockSpec((tq, HEAD_W), lambda b, h, i: (i, 0))
    tabk = pl.BlockSpec((seq, HEAD_W), lambda b, h, i: (0, 0))
    vec = pl.BlockSpec((1, HEAD_W), lambda b, h, i: (0, 0))
    return pl.pallas_call(
        functools.partial(_attn_kernel, seq=seq, out_scale=1.0 - lambda_init),
        grid=(bsz, N_HEADS, nq),
        in_specs=[qblk, kblk, kblk, tabq, tabq, tabk, tabk, vec, vec, vec,
                  pl.BlockSpec(memory_space=pltpu.SMEM),
                  pl.BlockSpec((HEAD_W, HEAD_W), lambda b, h, i: (0, 0))],
        out_specs=qblk,
        out_shape=jax.ShapeDtypeStruct((bsz * seq, MIX_W), F32),
        scratch_shapes=[pltpu.VMEM((n_kb, ATTN_KB, HEAD_W), BF16),
                        pltpu.VMEM((n_kb, HEAD_W + ATTN_ONES_ROWS, ATTN_KB), BF16),
                        pltpu.VMEM((2 * (tq // ATTN_TW), ATTN_TW, HEAD_W), BF16),
                        pltpu.VMEM((n_kb, ATTN_KB, ATTN_TW), F32),
                        pltpu.VMEM((n_kb, ATTN_KB, ATTN_TW), F32),
                        pltpu.VMEM((ATTN_KB, ATTN_TW), BF16),
                        pltpu.VMEM((ATTN_KB, ATTN_TW), BF16),
                        pltpu.VMEM((2 * (tq // ATTN_TW), HEAD_W + ATTN_ONES_ROWS, ATTN_TW), F32)],
        compiler_params=_cparams(("parallel", "parallel", "arbitrary")),
        name="diff_attention",
    )(q, k, v, cos_t, sin_t, cos_t, sin_t,
      jnp.tile(qg, 2).reshape(1, HEAD_W), jnp.tile(kg, 2).reshape(1, HEAD_W),
      sub_g.reshape(1, HEAD_W), jnp.reshape(lam, (1, 1)).astype(F32), jnp.asarray(seg, BF16))


def _rope_tables(seq):
    inv = 1.0 / (ROPE_THETA ** (jnp.arange(0, QK_DIM, 2, dtype=F32) / QK_DIM))
    ang = jnp.arange(seq, dtype=F32)[:, None] * inv[None, :]
    cos, sin = jnp.cos(ang), jnp.sin(ang)
    return jnp.tile(cos, (1, 4)), jnp.tile(jnp.concatenate([-sin, sin], axis=-1), (1, 2))


def kernel(x, norm_mix_g, norm_mlp_g, w_in_even, w_out_even, sgu_norm_g, sgu_w, sgu_b,
           hgrn_lb_logits, hgrn_norm_g, w_in_odd, w_out_odd, conv_w, q_norm_g, k_norm_g,
           lambda_q1, lambda_k1, lambda_q2, lambda_k2, diff_norm_g, mlp_w1, mlp_w2):
    bsz, seq, d = x.shape
    depth = norm_mix_g.shape[0]
    cos_t, sin_t = _rope_tables(seq)
    p_lb = jax.nn.softmax(hgrn_lb_logits.astype(F32), axis=1)
    lower_bounds = jnp.cumsum(p_lb, axis=1) - p_lb[:, :1]

    w1_bf, w2_bf = mlp_w1.astype(BF16), mlp_w2.astype(BF16)
    wo_even_bf, wo_odd_bf = w_out_even.astype(BF16), w_out_odd.astype(BF16)

    x2 = x.reshape(bsz * seq, d)
    for l in range(depth):
        if l % 2 == 0:
            e = l // 2
            u, v, q, i, g, a_fwd, a_bwd = norm_proj(x2, norm_mix_g[l], w_in_even, e)
            b_full = jnp.broadcast_to(sgu_b[e][:, :, None], (N_HEADS, SGU_CHUNK, HEAD_W))
            o_sum = hgrn2(q, i, a_fwd, a_bwd, lower_bounds[:, e], bsz, seq)
            x2 = mix_mlp(x2, (u, v, o_sum, g), wo_even_bf, e, norm_mlp_g[l], w1_bf, w2_bf, l, seq,
                         sgu_params=(sgu_norm_g[e], sgu_w[e].astype(BF16), b_full, hgrn_norm_g[e]))
        else:
            o = l // 2
            h_in, b_gate, c_gate, q, k, v = norm_proj(x2, norm_mix_g[l], w_in_odd, o)
            lambda_init = 0.8 - 0.6 * math.exp(-0.3 * l)
            lam = (jnp.exp(jnp.sum(lambda_q1[o] * lambda_k1[o]))
                   - jnp.exp(jnp.sum(lambda_q2[o] * lambda_k2[o])) + lambda_init)
            out_d = diff_attention(q, k, v, cos_t, sin_t, q_norm_g[o], k_norm_g[o], diff_norm_g[o],
                                   lam, lambda_init, bsz, seq)
            x2 = mix_mlp(x2, (h_in, b_gate, c_gate, out_d), wo_odd_bf, o, norm_mlp_g[l], w1_bf, w2_bf, l,
                         seq, conv_w=conv_w[o])
    return x2.reshape(bsz, seq, d)
```

```python
import functools
import math

import numpy as np
import jax
import jax.numpy as jnp
from jax import lax
from jax.experimental import pallas as pl
from jax.experimental.pallas import tpu as pltpu

F32 = jnp.float32
BF16 = jnp.bfloat16

EPS = 1e-6
ROPE_THETA = 10000.0
LOG2E = 1.4426950408889634

HEAD_W = 128
N_HEADS = 4
MIX_W = HEAD_W * N_HEADS
SGU_CHUNK = 128
HGRN_CHUNK = 128
QK_DIM = 64

VMEM_LIMIT = 56 * 1024 * 1024


def _cparams(sem):
    return pltpu.CompilerParams(dimension_semantics=sem, vmem_limit_bytes=VMEM_LIMIT)


def _dot(a, b):
    return jnp.dot(a, b, preferred_element_type=F32)


def _dot_nt(a, b):
    return lax.dot_general(a, b, (((1,), (1,)), ((), ())), preferred_element_type=F32)


def _split2(x):
    hi = x.astype(BF16)
    lo = (x - hi.astype(F32)).astype(BF16)
    return hi, lo


def _rms_rows(x, g):
    ms = jnp.mean(x * x, axis=-1, keepdims=True)
    return x * lax.rsqrt(ms + EPS) * g


def _norm_proj_kernel(x_ref, g_ref, w_ref, *rest):
    o_refs, wbf_ref = rest[:-1], rest[-1]

    @pl.when(pl.program_id(0) == 0)
    def _():
        for s in range(len(o_refs)):
            cols = slice(s * MIX_W, (s + 1) * MIX_W)
            wbf_ref[:, cols] = w_ref[:, cols].astype(BF16)

    h = _rms_rows(x_ref[...], g_ref[...]).astype(BF16)
    for s, o_ref in enumerate(o_refs):
        o_ref[...] = _dot(h, wbf_ref[:, s * MIX_W:(s + 1) * MIX_W])


def norm_proj(x2, g, w_stack, layer, tm=512):
    t, d = x2.shape
    n = w_stack.shape[2]
    n_split = n // MIX_W
    return pl.pallas_call(
        _norm_proj_kernel,
        grid=(t // tm,),
        in_specs=[
            pl.BlockSpec((tm, d), lambda i: (i, 0)),
            pl.BlockSpec((1, d), lambda i: (0, 0)),
            pl.BlockSpec((None, d, n), lambda i: (layer, 0, 0), pipeline_mode=pl.Buffered(1)),
        ],
        out_specs=[pl.BlockSpec((tm, MIX_W), lambda i: (i, 0)) for _ in range(n_split)],
        out_shape=[jax.ShapeDtypeStruct((t, MIX_W), F32) for _ in range(n_split)],
        scratch_shapes=[pltpu.VMEM((d, n), BF16)],
        compiler_params=_cparams(("arbitrary",)),
        name="norm_proj",
    )(x2, g.reshape(1, d), w_stack)


def _head_norm_gate(o, g, ng):
    outs = []
    for h in range(N_HEADS):
        cols = slice(h * HEAD_W, (h + 1) * HEAD_W)
        gh = g[:, cols]
        outs.append(_rms_rows(o[:, cols], ng[:, cols]) * (gh * (1.0 / (1.0 + jnp.exp(-gh)))))
    return jnp.concatenate(outs, axis=-1)


def _sgu_block(u, v, g, w_ref, b_ref):
    outs = []
    for h in range(N_HEADS):
        cols = slice(h * HEAD_W, (h + 1) * HEAD_W)
        vn = _rms_rows(jax.nn.gelu(v[:, cols]), g[:, cols])
        mixed = _dot(w_ref[h], vn.astype(BF16)) + b_ref[h]
        outs.append(jax.nn.gelu(u[:, cols]) * mixed)
    return jnp.concatenate(outs, axis=-1)


CONV_HALO = 8


def _mix_mlp_kernel(*refs, even, n_tiles, seq):
    if even:
        (x_ref, u_ref, v_ref, b_ref, gate_ref, sg_ref, ws_ref, bs_ref, ng_ref,
         wo_ref, g_ref, w1_ref, w2_ref, o_ref, h0_ref, h1_ref, acc0_ref, acc1_ref) = refs
    else:
        (x_ref, hin_ref, bg_ref, cg_ref, b_ref, hprev_ref, cprev_ref, hnext_ref, cnext_ref, cw_ref,
         wo_ref, g_ref, w1_ref, w2_ref, o_ref, h0_ref, h1_ref, acc0_ref, acc1_ref) = refs
    h_refs, acc_refs = (h0_ref, h1_ref), (acc0_ref, acc1_ref)
    i, j = pl.program_id(0), pl.program_id(1)
    rb = x_ref.shape[0]
    rows = pl.ds(pl.multiple_of(j * rb, rb), rb)

    def conv_block():
        pos = ((jnp.minimum(i, n_tiles - 1) * pl.num_programs(1) + j) * rb) % seq
        z = cg_ref[...] * hin_ref[...]
        z_before = (cprev_ref[...] * hprev_ref[...])[CONV_HALO - 1:CONV_HALO, :]
        z_after = (cnext_ref[...] * hnext_ref[...])[0:1, :]
        z_before = jnp.where(pos == 0, 0.0, z_before)
        z_after = jnp.where(pos + rb == seq, 0.0, z_after)
        r = lax.broadcasted_iota(jnp.int32, z.shape, 0)
        prev = jnp.where(r == 0, z_before, pltpu.roll(z, 1, 0))
        nxt = jnp.where(r == rb - 1, z_after, pltpu.roll(z, rb - 1, 0))
        y = cw_ref[0:1, :] * prev + cw_ref[1:2, :] * z + cw_ref[2:3, :] * nxt
        return bg_ref[...] * y

    def prepare(slot):
        if even:
            a = jnp.concatenate(
                [_sgu_block(u_ref[r:r + SGU_CHUNK, :], v_ref[r:r + SGU_CHUNK, :], sg_ref[...], ws_ref, bs_ref)
                 for r in range(0, rb, SGU_CHUNK)], axis=0)
            b = _head_norm_gate(b_ref[...], gate_ref[...], ng_ref[...])
        else:
            a, b = conv_block(), b_ref[...]
        x1 = x_ref[...] + _dot(a.astype(BF16), wo_ref[:MIX_W, :])
        x1 += _dot(b.astype(BF16), wo_ref[MIX_W:, :])
        acc_refs[slot][rows, :] = x1
        h_refs[slot][rows, :] = _rms_rows(x1, g_ref[...]).astype(BF16)

    def mlp_chunk(slot):
        hm = _dot(h_refs[slot][...], w1_ref[...])
        act = jnp.square(jnp.maximum(hm, 0.0)).astype(BF16)
        acc_refs[slot][...] += _dot(act, w2_ref[...])

    odd_step = (i % 2) == 1

    @pl.when(i == 0)
    def _():
        prepare(0)

    @pl.when((i > 0) & (i < n_tiles) & jnp.logical_not(odd_step))
    def _():
        mlp_chunk(1)
        prepare(0)

    @pl.when(odd_step)
    def _():
        mlp_chunk(0)
        prepare(1)

    @pl.when(i == n_tiles)
    def _():
        mlp_chunk(1)

    last = j == pl.num_programs(1) - 1

    @pl.when(last & odd_step)
    def _():
        o_ref[...] = acc0_ref[...]

    @pl.when(last & (i > 0) & jnp.logical_not(odd_step))
    def _():
        o_ref[...] = acc1_ref[...]


def mix_mlp(x2, mix_in, wo_bf, wo_layer, g, w1_bf, w2_bf, layer, seq, sgu_params=None, conv_w=None,
            tm=1024, tf=1024):
    t, d = x2.shape
    ff = w1_bf.shape[2]
    n_tiles, n_j = t // tm, ff // tf
    rb = tm // n_j
    assert n_tiles % 2 == 0 and rb % SGU_CHUNK == 0 and seq % rb == 0
    even = sgu_params is not None

    def row_block(i, j):
        return jnp.minimum(i, n_tiles - 1) * n_j + j

    def blk(width):
        return pl.BlockSpec((rb, width), lambda i, j: (row_block(i, j), 0))

    def const(shape):
        return pl.BlockSpec(shape, lambda i, j: (0,) * len(shape))

    ins = [x2] + list(mix_in)
    specs = [blk(d)] + [blk(MIX_W) for _ in mix_in]
    if even:
        norm_g, w_s_bf, b_full, hgrn_norm_g = sgu_params
        ins += [norm_g.reshape(1, MIX_W), w_s_bf, b_full, hgrn_norm_g.reshape(1, MIX_W)]
        specs += [const((1, MIX_W)), const((N_HEADS, SGU_CHUNK, SGU_CHUNK)),
                  const((N_HEADS, SGU_CHUNK, HEAD_W)), const((1, MIX_W))]
    else:
        h_in, _, c_gate, _ = mix_in
        per = rb // CONV_HALO
        before = pl.BlockSpec((CONV_HALO, MIX_W),
                              lambda i, j: (jnp.maximum(row_block(i, j) * per - 1, 0), 0))
        after = pl.BlockSpec((CONV_HALO, MIX_W),
                             lambda i, j: (jnp.minimum((row_block(i, j) + 1) * per, t // CONV_HALO - 1), 0))
        ins += [h_in, c_gate, h_in, c_gate, conv_w]
        specs += [before, before, after, after, const((3, MIX_W))]
    return pl.pallas_call(
        functools.partial(_mix_mlp_kernel, even=even, n_tiles=n_tiles, seq=seq),
        grid=(n_tiles + 1, n_j),
        in_specs=specs + [
            pl.BlockSpec((None, 2 * MIX_W, d), lambda i, j: (wo_layer, 0, 0)),
            const((1, d)),
            pl.BlockSpec((None, d, tf), lambda i, j: (layer, 0, j)),
            pl.BlockSpec((None, tf, d), lambda i, j: (layer, j, 0)),
        ],
        out_specs=pl.BlockSpec((tm, d), lambda i, j: (jnp.maximum(i - 1, 0), 0)),
        out_shape=jax.ShapeDtypeStruct((t, d), F32),
        scratch_shapes=[pltpu.VMEM((tm, d), BF16), pltpu.VMEM((tm, d), BF16),
                        pltpu.VMEM((tm, d), F32), pltpu.VMEM((tm, d), F32)],
        compiler_params=_cparams(("arbitrary", "arbitrary")),
        name="mix_mlp",
    )(*ins, wo_bf, g.reshape(1, d), w1_bf, w2_bf)


SUBLANES = 8


def _hgrn_tables(c):
    levels = int(math.log2(c))
    assert 1 << levels == c
    pos = np.arange(c)
    t = pos[:, None]
    i = pos[None, :]
    d_rows, m_rows = [], []
    for lv in range(levels):
        h = c >> (lv + 1)
        start = (t // (2 * h)) * (2 * h)
        mid = start + h - 1
        upper = (t - start) >= h
        if h < SUBLANES:
            d_rows.append(np.where(upper, (i > mid) & (i <= t), (i > t) & (i <= mid)))
        same = (t // (2 * h)) == (i // (2 * h))
        m_rows.append(same & upper & ((i % (2 * h)) < h))
    n_fine = len(d_rows)
    d_rows.append(i <= t)
    d = np.concatenate(d_rows, axis=0).astype(np.float32)
    m = np.concatenate(m_rows, axis=0).astype(np.float32)
    return d, m, levels, n_fine


def _flip_blocks(a, c):
    blocks = a.reshape(-1, c, c)
    return blocks[:, ::-1, ::-1].reshape(a.shape)


def _coarse_level(qh, kh, inc, h, fwd):
    c = inc.shape[0]
    zeros = jnp.zeros((h, inc.shape[1]), F32)
    q_rows, k_rows = [], []
    for n in range(c // (2 * h)):
        lo = slice(n * 2 * h, n * 2 * h + h)
        up = slice(n * 2 * h + h, (n + 1) * 2 * h)
        if fwd:
            ref = inc[n * 2 * h + h - 1:n * 2 * h + h, :]
            q_rows += [zeros, qh[up] * jnp.exp2(inc[up] - ref)]
            k_rows += [kh[lo] * jnp.exp2(ref - inc[lo]), zeros]
        else:
            ref = inc[n * 2 * h + h:n * 2 * h + h + 1, :]
            q_rows += [qh[lo] * jnp.exp2(inc[lo] - ref), zeros]
            k_rows += [zeros, kh[up] * jnp.exp2(ref - inc[up])]
    return jnp.concatenate(q_rows, axis=0), jnp.concatenate(k_rows, axis=0)


def _hgrn_direction(q, v_bf, vt_bf, a, lb, d_ref, m_ref, st_ref, *, levels, n_fine, fwd):
    c = q.shape[0]
    f = lb + (1.0 - lb) * (1.0 / (1.0 + jnp.exp(-a)))
    lf = jnp.log(f) * LOG2E
    kk = 1.0 - f
    hi, lo = _split2(lf)
    sums = _dot(d_ref[...], hi) + _dot(d_ref[...], lo)
    e_all = jnp.exp2(sums)
    inc_rows = slice(n_fine * c, (n_fine + 1) * c)
    last_row = c - 1 if fwd else 0
    row = lax.broadcasted_iota(jnp.int32, (c, c), 0)
    col = lax.broadcasted_iota(jnp.int32, (c, c), 1)
    eye = row == col
    outs = []
    for h in range(N_HEADS):
        cols = slice(h * HEAD_W, (h + 1) * HEAD_W)
        qh, kh, inc = q[:, cols], kk[:, cols], sums[inc_rows, cols]
        att = jnp.where(eye, jnp.sum(qh * kh, axis=-1, keepdims=True), 0.0)
        for lv in range(levels):
            half = c >> (lv + 1)
            if half >= SUBLANES:
                ql, kl = _coarse_level(qh, kh, inc, half, fwd)
            else:
                fine = lv - (levels - n_fine)
                e_lv = e_all[fine * c:(fine + 1) * c, cols]
                ql, kl = qh * e_lv, kh * e_lv
            pair = _dot_nt(ql.astype(BF16), kl.astype(BF16))
            att += pair if lv == 0 else pair * m_ref[lv * c:(lv + 1) * c, :]
        e_inc = e_all[inc_rows, cols]
        st = st_ref[h]
        o_h = _dot(att.astype(BF16), v_bf[:, cols])
        o_h += _dot_nt((qh * e_inc).astype(BF16), st.astype(BF16))
        k_dec = (kh * jnp.exp2(inc[last_row:last_row + 1, :] - inc)).astype(BF16)
        st_ref[h] = st * e_inc[last_row:last_row + 1, :] + _dot(vt_bf[cols, :], k_dec)
        outs.append(o_h)
    return jnp.concatenate(outs, axis=-1)


HGRN_STEP_CHUNKS = 4


def _hgrn_kernel(qf_ref, qb_ref, if_ref, ib_ref, af_ref, ab_ref, lbf_ref, lbb_ref,
                 df_ref, db_ref, mf_ref, mb_ref, o_ref, stf_ref, stb_ref, *, levels, n_fine, chunk):
    n = pl.program_id(1)
    n_steps = pl.num_programs(1)
    blk = HGRN_STEP_CHUNKS * chunk

    @pl.when(n == 0)
    def _():
        stf_ref[...] = jnp.zeros_like(stf_ref)
        stb_ref[...] = jnp.zeros_like(stb_ref)

    def run(q_ref, i_ref, a_ref, lb_ref, d_ref, m_ref, st_ref, sub, fwd):
        rows = slice(sub * chunk, (sub + 1) * chunk)
        v = i_ref[rows, :]
        return _hgrn_direction(q_ref[rows, :], v.astype(BF16), v.T.astype(BF16), a_ref[rows, :],
                               lb_ref[...], d_ref, m_ref, st_ref, levels=levels, n_fine=n_fine, fwd=fwd)

    o_f, o_b = [], [None] * HGRN_STEP_CHUNKS
    for sub in range(HGRN_STEP_CHUNKS):
        rev = HGRN_STEP_CHUNKS - 1 - sub
        o_f.append(run(qf_ref, if_ref, af_ref, lbf_ref, df_ref, mf_ref, stf_ref, sub, True))
        o_b[rev] = run(qb_ref, ib_ref, ab_ref, lbb_ref, db_ref, mb_ref, stb_ref, rev, False)
    o_f = jnp.concatenate(o_f, axis=0)
    o_b = jnp.concatenate(o_b, axis=0)
    rows_f = pl.ds(pl.multiple_of(n * blk, blk), blk)
    rows_b = pl.ds(pl.multiple_of((n_steps - 1 - n) * blk, blk), blk)

    @pl.when(2 * n < n_steps)
    def _():
        o_ref[rows_f, :] = o_f
        o_ref[rows_b, :] = o_b

    @pl.when(2 * n >= n_steps)
    def _():
        o_ref[rows_f, :] += o_f
        o_ref[rows_b, :] += o_b


def hgrn2(q, i, a_fwd, a_bwd, lb, bsz, seq, chunk=HGRN_CHUNK):
    blk = HGRN_STEP_CHUNKS * chunk
    n_steps = seq // blk
    assert n_steps % 2 == 0
    d_np, m_np, levels, n_fine = _hgrn_tables(chunk)
    d_f = jnp.asarray(d_np, BF16)
    d_b = jnp.asarray(_flip_blocks(d_np, chunk), BF16)
    m_f = jnp.asarray(m_np, F32)
    m_b = jnp.asarray(_flip_blocks(m_np, chunk), F32)

    fwd = pl.BlockSpec((blk, MIX_W), lambda b, n: (b * n_steps + n, 0))
    bwd = pl.BlockSpec((blk, MIX_W), lambda b, n: (b * n_steps + n_steps - 1 - n, 0))
    vec = pl.BlockSpec((1, MIX_W), lambda b, n: (0, 0))

    def table(a):
        return pl.BlockSpec(a.shape, lambda b, n: (0, 0))

    return pl.pallas_call(
        functools.partial(_hgrn_kernel, levels=levels, n_fine=n_fine, chunk=chunk),
        grid=(bsz, n_steps),
        in_specs=[fwd, bwd, fwd, bwd, fwd, bwd, vec, vec,
                  table(d_f), table(d_b), table(m_f), table(m_b)],
        out_specs=pl.BlockSpec((seq, MIX_W), lambda b, n: (b, 0)),
        out_shape=jax.ShapeDtypeStruct((bsz * seq, MIX_W), F32),
        scratch_shapes=[pltpu.VMEM((N_HEADS, HEAD_W, HEAD_W), F32),
                        pltpu.VMEM((N_HEADS, HEAD_W, HEAD_W), F32)],
        compiler_params=_cparams(("parallel", "arbitrary")),
        name="hgrn2",
    )(q, q, i, i, a_fwd, a_bwd, lb[0:1], lb[1:2], d_f, d_b, m_f, m_b)


ATTN_TQ = 2048
ATTN_TW = 512


def _qk_prep(x, gain, cos, sin_signed, seg_mean):
    hi, lo = _split2(x * x)
    ms = _dot(hi, seg_mean) + _dot(lo, seg_mean)
    y = x * lax.rsqrt(ms + EPS) * gain
    lane = lax.broadcasted_iota(jnp.int32, y.shape, 1)
    half = QK_DIM // 2
    partner = jnp.where((lane % QK_DIM) < half,
                        pltpu.roll(y, HEAD_W - half, 1), pltpu.roll(y, half, 1))
    return y * cos + partner * sin_signed


ATTN_ONES_ROWS = 16
ATTN_KB = 2048


def _attn_kernel(q_ref, k_ref, v_ref, cosq_ref, sinq_ref, cosk_ref, sink_ref, qg_ref, kg_ref,
                 sg_ref, lam_ref, mean_ref, o_ref, kh_ref, v1t_ref, qp_ref, s0_ref, s1_ref, pa_ref, pb_ref,
                 acc_ref, *, seq, out_scale):
    qi = pl.program_id(2)
    seg_mean = mean_ref[...]
    n_kb = seq // ATTN_KB

    @pl.when(qi == 0)
    def _():
        def prep(r, carry):
            rows = pl.ds(pl.multiple_of(r * ATTN_KB, ATTN_KB), ATTN_KB)
            kh_ref[r] = _qk_prep(k_ref[rows, :], kg_ref[...], cosk_ref[rows, :],
                                 sink_ref[rows, :], seg_mean).astype(BF16)
            v1t_ref[r, 0:HEAD_W, :] = v_ref[rows, :].T.astype(BF16)
            v1t_ref[r, HEAD_W:HEAD_W + ATTN_ONES_ROWS, :] = jnp.ones((ATTN_ONES_ROWS, ATTN_KB), BF16)
            return carry

        lax.fori_loop(0, n_kb, prep, 0)

    tw = ATTN_TW
    n_units = 2 * (q_ref.shape[0] // tw)
    scale = (QK_DIM ** -0.5) * LOG2E
    for t in range(n_units // 2):
        rows = slice(t * tw, (t + 1) * tw)
        qh = _qk_prep(q_ref[rows, :], qg_ref[...], cosq_ref[rows, :], sinq_ref[rows, :], seg_mean) * scale
        lane = lax.broadcasted_iota(jnp.int32, qh.shape, 1)
        qp_ref[2 * t] = jnp.where(lane < QK_DIM, qh, 0.0).astype(BF16)
        qp_ref[2 * t + 1] = jnp.where(lane >= QK_DIM, qh, 0.0).astype(BF16)

    s_refs, p_refs = (s0_ref, s1_ref), (pa_ref, pb_ref)
    neg = jnp.full((8, tw), -jnp.inf, F32)

    def phase(us, ue, m_e):
        def sc(kb, m8):
            if us is None:
                return m8
            s_t = _dot_nt(kh_ref[kb], qp_ref[us])
            s_refs[us % 2][kb] = s_t
            return jnp.maximum(m8, jnp.max(s_t.reshape(ATTN_KB // 8, 8, tw), axis=0))

        def pr(kb, slot):
            if ue is not None:
                p_refs[slot][...] = jnp.exp2(s_refs[ue % 2][kb] - m_e).astype(BF16)

        def va(kb, slot, first=False):
            if ue is not None:
                part = _dot(v1t_ref[kb], p_refs[slot][...])
                acc_ref[ue] = part if first else acc_ref[ue] + part

        m8 = neg
        for kb in range(n_kb):
            m8 = sc(kb, m8)
            pr(kb, kb % 2)
            if kb > 0:
                va(kb - 1, (kb - 1) % 2, first=(kb == 1))
        va(n_kb - 1, (n_kb - 1) % 2, first=(n_kb == 1))
        return jnp.max(m8, axis=0, keepdims=True)

    m = phase(0, None, None)
    for u in range(1, n_units):
        m = phase(u, u - 1, m)
    phase(None, n_units - 1, m)

    for t in range(n_units // 2):
        parts = [acc_ref[2 * t + c, 0:HEAD_W, :] / acc_ref[2 * t + c, HEAD_W:HEAD_W + 1, :]
                 for c in range(2)]
        o = (parts[0] - lam_ref[0, 0] * parts[1]).T
        o_ref[t * tw:(t + 1) * tw, :] = _rms_rows(o, sg_ref[...]) * out_scale


def diff_attention(q, k, v, cos_t, sin_t, qg, kg, sub_g, lam, lambda_init, bsz, seq, tq=ATTN_TQ):
    nq = seq // tq
    n_kb = seq // ATTN_KB
    assert seq % ATTN_KB == 0
    seg = np.kron(np.eye(2, dtype=np.float32), np.full((QK_DIM, QK_DIM), 1.0 / QK_DIM, np.float32))
    qblk = pl.BlockSpec((tq, HEAD_W), lambda b, h, i: (b * nq + i, h))
    kblk = pl.BlockSpec((seq, HEAD_W), lambda b, h, i: (b, h))
    tabq = pl.BlockSpec((tq, HEAD_W), lambda b, h, i: (i, 0))
    tabk = pl.BlockSpec((seq, HEAD_W), lambda b, h, i: (0, 0))
    vec = pl.BlockSpec((1, HEAD_W), lambda b, h, i: (0, 0))
    return pl.pallas_call(
        functools.partial(_attn_kernel, seq=seq, out_scale=1.0 - lambda_init),
        grid=(bsz, N_HEADS, nq),
        in_specs=[qblk, kblk, kblk, tabq, tabq, tabk, tabk, vec, vec, vec,
                  pl.BlockSpec(memory_space=pltpu.SMEM),
                  pl.BlockSpec((HEAD_W, HEAD_W), lambda b, h, i: (0, 0))],
        out_specs=qblk,
        out_shape=jax.ShapeDtypeStruct((bsz * seq, MIX_W), F32),
        scratch_shapes=[pltpu.VMEM((n_kb, ATTN_KB, HEAD_W), BF16),
                        pltpu.VMEM((n_kb, HEAD_W + ATTN_ONES_ROWS, ATTN_KB), BF16),
                        pltpu.VMEM((2 * (tq // ATTN_TW), ATTN_TW, HEAD_W), BF16),
                        pltpu.VMEM((n_kb, ATTN_KB, ATTN_TW), F32),
                        pltpu.VMEM((n_kb, ATTN_KB, ATTN_TW), F32),
                        pltpu.VMEM((ATTN_KB, ATTN_TW), BF16),
                        pltpu.VMEM((ATTN_KB, ATTN_TW), BF16),
                        pltpu.VMEM((2 * (tq // ATTN_TW), HEAD_W + ATTN_ONES_ROWS, ATTN_TW), F32)],
        compiler_params=_cparams(("parallel", "parallel", "arbitrary")),
        name="diff_attention",
    )(q, k, v, cos_t, sin_t, cos_t, sin_t,
      jnp.tile(qg, 2).reshape(1, HEAD_W), jnp.tile(kg, 2).reshape(1, HEAD_W),
      sub_g.reshape(1, HEAD_W), jnp.reshape(lam, (1, 1)).astype(F32), jnp.asarray(seg, BF16))


def _rope_tables(seq):
    inv = 1.0 / (ROPE_THETA ** (jnp.arange(0, QK_DIM, 2, dtype=F32) / QK_DIM))
    ang = jnp.arange(seq, dtype=F32)[:, None] * inv[None, :]
    cos, sin = jnp.cos(ang), jnp.sin(ang)
    return jnp.tile(cos, (1, 4)), jnp.tile(jnp.concatenate([-sin, sin], axis=-1), (1, 2))


def kernel(x, norm_mix_g, norm_mlp_g, w_in_even, w_out_even, sgu_norm_g, sgu_w, sgu_b,
           hgrn_lb_logits, hgrn_norm_g, w_in_odd, w_out_odd, conv_w, q_norm_g, k_norm_g,
           lambda_q1, lambda_k1, lambda_q2, lambda_k2, diff_norm_g, mlp_w1, mlp_w2):
    bsz, seq, d = x.shape
    depth = norm_mix_g.shape[0]
    cos_t, sin_t = _rope_tables(seq)
    p_lb = jax.nn.softmax(hgrn_lb_logits.astype(F32), axis=1)
    lower_bounds = jnp.cumsum(p_lb, axis=1) - p_lb[:, :1]

    w1_bf, w2_bf = mlp_w1.astype(BF16), mlp_w2.astype(BF16)
    wo_even_bf, wo_odd_bf = w_out_even.astype(BF16), w_out_odd.astype(BF16)

    x2 = x.reshape(bsz * seq, d)
    for l in range(depth):
        if l % 2 == 0:
            e = l // 2
            u, v, q, i, g, a_fwd, a_bwd = norm_proj(x2, norm_mix_g[l], w_in_even, e)
            b_full = jnp.broadcast_to(sgu_b[e][:, :, None], (N_HEADS, SGU_CHUNK, HEAD_W))
            o_sum = hgrn2(q, i, a_fwd, a_bwd, lower_bounds[:, e], bsz, seq)
            x2 = mix_mlp(x2, (u, v, o_sum, g), wo_even_bf, e, norm_mlp_g[l], w1_bf, w2_bf, l, seq,
                         sgu_params=(sgu_norm_g[e], sgu_w[e].astype(BF16), b_full, hgrn_norm_g[e]))
        else:
            o = l // 2
            h_in, b_gate, c_gate, q, k, v = norm_proj(x2, norm_mix_g[l], w_in_odd, o)
            lambda_init = 0.8 - 0.6 * math.exp(-0.3 * l)
            lam = (jnp.exp(jnp.sum(lambda_q1[o] * lambda_k1[o]))
                   - jnp.exp(jnp.sum(lambda_q2[o] * lambda_k2[o])) + lambda_init)
            out_d = diff_attention(q, k, v, cos_t, sin_t, q_norm_g[o], k_norm_g[o], diff_norm_g[o],
                                   lam, lambda_init, bsz, seq)
            x2 = mix_mlp(x2, (h_in, b_gate, c_gate, out_d), wo_odd_bf, o, norm_mlp_g[l], w1_bf, w2_bf, l,
                         seq, conv_w=conv_w[o])
    return x2.reshape(bsz, seq, d)
```

```python
import functools
import math

import numpy as np
import jax
import jax.numpy as jnp
from jax import lax
from jax.experimental import pallas as pl
from jax.experimental.pallas import tpu as pltpu

F32 = jnp.float32
BF16 = jnp.bfloat16

EPS = 1e-6
ROPE_THETA = 10000.0
LOG2E = 1.4426950408889634

HEAD_W = 128
N_HEADS = 4
MIX_W = HEAD_W * N_HEADS
SGU_CHUNK = 128
HGRN_CHUNK = 128
QK_DIM = 64

VMEM_LIMIT = 56 * 1024 * 1024


def _cparams(sem):
    return pltpu.CompilerParams(dimension_semantics=sem, vmem_limit_bytes=VMEM_LIMIT)


def _dot(a, b):
    return jnp.dot(a, b, preferred_element_type=F32)


def _dot_nt(a, b):
    return lax.dot_general(a, b, (((1,), (1,)), ((), ())), preferred_element_type=F32)


def _split2(x):
    hi = x.astype(BF16)
    lo = (x - hi.astype(F32)).astype(BF16)
    return hi, lo


def _rms_rows(x, g):
    ms = jnp.mean(x * x, axis=-1, keepdims=True)
    return x * lax.rsqrt(ms + EPS) * g


def _norm_proj_kernel(x_ref, g_ref, w_ref, *rest):
    o_refs, wbf_ref = rest[:-1], rest[-1]

    @pl.when(pl.program_id(0) == 0)
    def _():
        for s in range(len(o_refs)):
            cols = slice(s * MIX_W, (s + 1) * MIX_W)
            wbf_ref[:, cols] = w_ref[:, cols].astype(BF16)

    h = _rms_rows(x_ref[...], g_ref[...]).astype(BF16)
    for s, o_ref in enumerate(o_refs):
        o_ref[...] = _dot(h, wbf_ref[:, s * MIX_W:(s + 1) * MIX_W])


def norm_proj(x2, g, w_stack, layer, tm=512):
    t, d = x2.shape
    n = w_stack.shape[2]
    n_split = n // MIX_W
    return pl.pallas_call(
        _norm_proj_kernel,
        grid=(t // tm,),
        in_specs=[
            pl.BlockSpec((tm, d), lambda i: (i, 0)),
            pl.BlockSpec((1, d), lambda i: (0, 0)),
            pl.BlockSpec((None, d, n), lambda i: (layer, 0, 0), pipeline_mode=pl.Buffered(1)),
        ],
        out_specs=[pl.BlockSpec((tm, MIX_W), lambda i: (i, 0)) for _ in range(n_split)],
        out_shape=[jax.ShapeDtypeStruct((t, MIX_W), F32) for _ in range(n_split)],
        scratch_shapes=[pltpu.VMEM((d, n), BF16)],
        compiler_params=_cparams(("arbitrary",)),
        name="norm_proj",
    )(x2, g.reshape(1, d), w_stack)


def _head_norm_gate(o, g, ng):
    outs = []
    for h in range(N_HEADS):
        cols = slice(h * HEAD_W, (h + 1) * HEAD_W)
        gh = g[:, cols]
        outs.append(_rms_rows(o[:, cols], ng[:, cols]) * (gh * (1.0 / (1.0 + jnp.exp(-gh)))))
    return jnp.concatenate(outs, axis=-1)


def _sgu_block(u, v, g, w_ref, b_ref):
    outs = []
    for h in range(N_HEADS):
        cols = slice(h * HEAD_W, (h + 1) * HEAD_W)
        vn = _rms_rows(jax.nn.gelu(v[:, cols]), g[:, cols])
        mixed = _dot(w_ref[h], vn.astype(BF16)) + b_ref[h]
        outs.append(jax.nn.gelu(u[:, cols]) * mixed)
    return jnp.concatenate(outs, axis=-1)


CONV_HALO = 8


def _mix_mlp_kernel(*refs, even, n_tiles, seq):
    if even:
        (x_ref, u_ref, v_ref, b_ref, gate_ref, sg_ref, ws_ref, bs_ref, ng_ref,
         wo_ref, g_ref, w1_ref, w2_ref, o_ref, h0_ref, h1_ref, acc0_ref, acc1_ref) = refs
    else:
        (x_ref, hin_ref, bg_ref, cg_ref, b_ref, hprev_ref, cprev_ref, hnext_ref, cnext_ref, cw_ref,
         wo_ref, g_ref, w1_ref, w2_ref, o_ref, h0_ref, h1_ref, acc0_ref, acc1_ref) = refs
    h_refs, acc_refs = (h0_ref, h1_ref), (acc0_ref, acc1_ref)
    i, j = pl.program_id(0), pl.program_id(1)
    rb = x_ref.shape[0]
    rows = pl.ds(pl.multiple_of(j * rb, rb), rb)

    def conv_block():
        pos = ((jnp.minimum(i, n_tiles - 1) * pl.num_programs(1) + j) * rb) % seq
        z = cg_ref[...] * hin_ref[...]
        z_before = (cprev_ref[...] * hprev_ref[...])[CONV_HALO - 1:CONV_HALO, :]
        z_after = (cnext_ref[...] * hnext_ref[...])[0:1, :]
        z_before = jnp.where(pos == 0, 0.0, z_before)
        z_after = jnp.where(pos + rb == seq, 0.0, z_after)
        r = lax.broadcasted_iota(jnp.int32, z.shape, 0)
        prev = jnp.where(r == 0, z_before, pltpu.roll(z, 1, 0))
        nxt = jnp.where(r == rb - 1, z_after, pltpu.roll(z, rb - 1, 0))
        y = cw_ref[0:1, :] * prev + cw_ref[1:2, :] * z + cw_ref[2:3, :] * nxt
        return bg_ref[...] * y

    def prepare(slot):
        if even:
            a = jnp.concatenate(
                [_sgu_block(u_ref[r:r + SGU_CHUNK, :], v_ref[r:r + SGU_CHUNK, :], sg_ref[...], ws_ref, bs_ref)
                 for r in range(0, rb, SGU_CHUNK)], axis=0)
            b = _head_norm_gate(b_ref[...], gate_ref[...], ng_ref[...])
        else:
            a, b = conv_block(), b_ref[...]
        x1 = x_ref[...] + _dot(a.astype(BF16), wo_ref[:MIX_W, :])
        x1 += _dot(b.astype(BF16), wo_ref[MIX_W:, :])
        acc_refs[slot][rows, :] = x1
        h_refs[slot][rows, :] = _rms_rows(x1, g_ref[...]).astype(BF16)

    def mlp_chunk(slot):
        hm = _dot(h_refs[slot][...], w1_ref[...])
        act = jnp.square(jnp.maximum(hm, 0.0)).astype(BF16)
        acc_refs[slot][...] += _dot(act, w2_ref[...])

    odd_step = (i % 2) == 1

    @pl.when(i == 0)
    def _():
        prepare(0)

    @pl.when((i > 0) & (i < n_tiles) & jnp.logical_not(odd_step))
    def _():
        mlp_chunk(1)
        prepare(0)

    @pl.when(odd_step)
    def _():
        mlp_chunk(0)
        prepare(1)

    @pl.when(i == n_tiles)
    def _():
        mlp_chunk(1)

    last = j == pl.num_programs(1) - 1

    @pl.when(last & odd_step)
    def _():
        o_ref[...] = acc0_ref[...]

    @pl.when(last & (i > 0) & jnp.logical_not(odd_step))
    def _():
        o_ref[...] = acc1_ref[...]


def mix_mlp(x2, mix_in, wo_bf, wo_layer, g, w1_bf, w2_bf, layer, seq, sgu_params=None, conv_w=None,
            tm=1024, tf=2048):
    t, d = x2.shape
    ff = w1_bf.shape[2]
    n_tiles, n_j = t // tm, ff // tf
    rb = tm // n_j
    assert n_tiles % 2 == 0 and rb % SGU_CHUNK == 0 and seq % rb == 0
    even = sgu_params is not None

    def row_block(i, j):
        return jnp.minimum(i, n_tiles - 1) * n_j + j

    def blk(width):
        return pl.BlockSpec((rb, width), lambda i, j: (row_block(i, j), 0))

    def const(shape):
        return pl.BlockSpec(shape, lambda i, j: (0,) * len(shape))

    ins = [x2] + list(mix_in)
    specs = [blk(d)] + [blk(MIX_W) for _ in mix_in]
    if even:
        norm_g, w_s_bf, b_full, hgrn_norm_g = sgu_params
        ins += [norm_g.reshape(1, MIX_W), w_s_bf, b_full, hgrn_norm_g.reshape(1, MIX_W)]
        specs += [const((1, MIX_W)), const((N_HEADS, SGU_CHUNK, SGU_CHUNK)),
                  const((N_HEADS, SGU_CHUNK, HEAD_W)), const((1, MIX_W))]
    else:
        h_in, _, c_gate, _ = mix_in
        per = rb // CONV_HALO
        before = pl.BlockSpec((CONV_HALO, MIX_W),
                              lambda i, j: (jnp.maximum(row_block(i, j) * per - 1, 0), 0))
        after = pl.BlockSpec((CONV_HALO, MIX_W),
                             lambda i, j: (jnp.minimum((row_block(i, j) + 1) * per, t // CONV_HALO - 1), 0))
        ins += [h_in, c_gate, h_in, c_gate, conv_w]
        specs += [before, before, after, after, const((3, MIX_W))]
    return pl.pallas_call(
        functools.partial(_mix_mlp_kernel, even=even, n_tiles=n_tiles, seq=seq),
        grid=(n_tiles + 1, n_j),
        in_specs=specs + [
            pl.BlockSpec((None, 2 * MIX_W, d), lambda i, j: (wo_layer, 0, 0)),
            const((1, d)),
            pl.BlockSpec((None, d, tf), lambda i, j: (layer, 0, j)),
            pl.BlockSpec((None, tf, d), lambda i, j: (layer, j, 0)),
        ],
        out_specs=pl.BlockSpec((tm, d), lambda i, j: (jnp.maximum(i - 1, 0), 0)),
        out_shape=jax.ShapeDtypeStruct((t, d), F32),
        scratch_shapes=[pltpu.VMEM((tm, d), BF16), pltpu.VMEM((tm, d), BF16),
                        pltpu.VMEM((tm, d), F32), pltpu.VMEM((tm, d), F32)],
        compiler_params=_cparams(("arbitrary", "arbitrary")),
        name="mix_mlp",
    )(*ins, wo_bf, g.reshape(1, d), w1_bf, w2_bf)


SUBLANES = 8


def _hgrn_tables(c):
    levels = int(math.log2(c))
    assert 1 << levels == c
    pos = np.arange(c)
    t = pos[:, None]
    i = pos[None, :]
    d_rows, m_rows = [], []
    for lv in range(levels):
        h = c >> (lv + 1)
        start = (t // (2 * h)) * (2 * h)
        mid = start + h - 1
        upper = (t - start) >= h
        if h < SUBLANES:
            d_rows.append(np.where(upper, (i > mid) & (i <= t), (i > t) & (i <= mid)))
        same = (t // (2 * h)) == (i // (2 * h))
        m_rows.append(same & upper & ((i % (2 * h)) < h))
    n_fine = len(d_rows)
    d_rows.append(i <= t)
    d = np.concatenate(d_rows, axis=0).astype(np.float32)
    m = np.concatenate(m_rows, axis=0).astype(np.float32)
    return d, m, levels, n_fine


def _flip_blocks(a, c):
    blocks = a.reshape(-1, c, c)
    return blocks[:, ::-1, ::-1].reshape(a.shape)


def _coarse_level(qh, kh, inc, h, fwd):
    c = inc.shape[0]
    zeros = jnp.zeros((h, inc.shape[1]), F32)
    q_rows, k_rows = [], []
    for n in range(c // (2 * h)):
        lo = slice(n * 2 * h, n * 2 * h + h)
        up = slice(n * 2 * h + h, (n + 1) * 2 * h)
        if fwd:
            ref = inc[n * 2 * h + h - 1:n * 2 * h + h, :]
            q_rows += [zeros, qh[up] * jnp.exp2(inc[up] - ref)]
            k_rows += [kh[lo] * jnp.exp2(ref - inc[lo]), zeros]
        else:
            ref = inc[n * 2 * h + h:n * 2 * h + h + 1, :]
            q_rows += [qh[lo] * jnp.exp2(inc[lo] - ref), zeros]
            k_rows += [zeros, kh[up] * jnp.exp2(ref - inc[up])]
    return jnp.concatenate(q_rows, axis=0), jnp.concatenate(k_rows, axis=0)


def _hgrn_direction(q, v_bf, vt_bf, a, lb, d_ref, m_ref, st_ref, *, levels, n_fine, fwd):
    c = q.shape[0]
    f = lb + (1.0 - lb) * (1.0 / (1.0 + jnp.exp(-a)))
    lf = jnp.log(f) * LOG2E
    kk = 1.0 - f
    hi, lo = _split2(lf)
    sums = _dot(d_ref[...], hi) + _dot(d_ref[...], lo)
    e_all = jnp.exp2(sums)
    inc_rows = slice(n_fine * c, (n_fine + 1) * c)
    last_row = c - 1 if fwd else 0
    row = lax.broadcasted_iota(jnp.int32, (c, c), 0)
    col = lax.broadcasted_iota(jnp.int32, (c, c), 1)
    eye = row == col
    outs = []
    for h in range(N_HEADS):
        cols = slice(h * HEAD_W, (h + 1) * HEAD_W)
        qh, kh, inc = q[:, cols], kk[:, cols], sums[inc_rows, cols]
        att = jnp.where(eye, jnp.sum(qh * kh, axis=-1, keepdims=True), 0.0)
        for lv in range(levels):
            half = c >> (lv + 1)
            if half >= SUBLANES:
                ql, kl = _coarse_level(qh, kh, inc, half, fwd)
            else:
                fine = lv - (levels - n_fine)
                e_lv = e_all[fine * c:(fine + 1) * c, cols]
                ql, kl = qh * e_lv, kh * e_lv
            pair = _dot_nt(ql.astype(BF16), kl.astype(BF16))
            att += pair if lv == 0 else pair * m_ref[lv * c:(lv + 1) * c, :]
        e_inc = e_all[inc_rows, cols]
        st = st_ref[h]
        o_h = _dot(att.astype(BF16), v_bf[:, cols])
        o_h += _dot_nt((qh * e_inc).astype(BF16), st.astype(BF16))
        k_dec = (kh * jnp.exp2(inc[last_row:last_row + 1, :] - inc)).astype(BF16)
        st_ref[h] = st * e_inc[last_row:last_row + 1, :] + _dot(vt_bf[cols, :], k_dec)
        outs.append(o_h)
    return jnp.concatenate(outs, axis=-1)


HGRN_STEP_CHUNKS = 8


def _hgrn_kernel(qf_ref, qb_ref, if_ref, ib_ref, af_ref, ab_ref, lbf_ref, lbb_ref,
                 df_ref, db_ref, mf_ref, mb_ref, o_ref, stf_ref, stb_ref, *, levels, n_fine, chunk):
    n = pl.program_id(1)
    n_steps = pl.num_programs(1)
    blk = HGRN_STEP_CHUNKS * chunk

    @pl.when(n == 0)
    def _():
        stf_ref[...] = jnp.zeros_like(stf_ref)
        stb_ref[...] = jnp.zeros_like(stb_ref)

    def run(q_ref, i_ref, a_ref, lb_ref, d_ref, m_ref, st_ref, sub, fwd):
        rows = slice(sub * chunk, (sub + 1) * chunk)
        v = i_ref[rows, :]
        return _hgrn_direction(q_ref[rows, :], v.astype(BF16), v.T.astype(BF16), a_ref[rows, :],
                               lb_ref[...], d_ref, m_ref, st_ref, levels=levels, n_fine=n_fine, fwd=fwd)

    o_f, o_b = [], [None] * HGRN_STEP_CHUNKS
    for sub in range(HGRN_STEP_CHUNKS):
        rev = HGRN_STEP_CHUNKS - 1 - sub
        o_f.append(run(qf_ref, if_ref, af_ref, lbf_ref, df_ref, mf_ref, stf_ref, sub, True))
        o_b[rev] = run(qb_ref, ib_ref, ab_ref, lbb_ref, db_ref, mb_ref, stb_ref, rev, False)
    o_f = jnp.concatenate(o_f, axis=0)
    o_b = jnp.concatenate(o_b, axis=0)
    rows_f = pl.ds(pl.multiple_of(n * blk, blk), blk)
    rows_b = pl.ds(pl.multiple_of((n_steps - 1 - n) * blk, blk), blk)

    @pl.when(2 * n < n_steps)
    def _():
        o_ref[rows_f, :] = o_f
        o_ref[rows_b, :] = o_b

    @pl.when(2 * n >= n_steps)
    def _():
        o_ref[rows_f, :] += o_f
        o_ref[rows_b, :] += o_b


def hgrn2(q, i, a_fwd, a_bwd, lb, bsz, seq, chunk=HGRN_CHUNK):
    blk = HGRN_STEP_CHUNKS * chunk
    n_steps = seq // blk
    assert n_steps % 2 == 0
    d_np, m_np, levels, n_fine = _hgrn_tables(chunk)
    d_f = jnp.asarray(d_np, BF16)
    d_b = jnp.asarray(_flip_blocks(d_np, chunk), BF16)
    m_f = jnp.asarray(m_np, F32)
    m_b = jnp.asarray(_flip_blocks(m_np, chunk), F32)

    fwd = pl.BlockSpec((blk, MIX_W), lambda b, n: (b * n_steps + n, 0))
    bwd = pl.BlockSpec((blk, MIX_W), lambda b, n: (b * n_steps + n_steps - 1 - n, 0))
    vec = pl.BlockSpec((1, MIX_W), lambda b, n: (0, 0))

    def table(a):
        return pl.BlockSpec(a.shape, lambda b, n: (0, 0))

    return pl.pallas_call(
        functools.partial(_hgrn_kernel, levels=levels, n_fine=n_fine, chunk=chunk),
        grid=(bsz, n_steps),
        in_specs=[fwd, bwd, fwd, bwd, fwd, bwd, vec, vec,
                  table(d_f), table(d_b), table(m_f), table(m_b)],
        out_specs=pl.BlockSpec((seq, MIX_W), lambda b, n: (b, 0)),
        out_shape=jax.ShapeDtypeStruct((bsz * seq, MIX_W), F32),
        scratch_shapes=[pltpu.VMEM((N_HEADS, HEAD_W, HEAD_W), F32),
                        pltpu.VMEM((N_HEADS, HEAD_W, HEAD_W), F32)],
        compiler_params=_cparams(("parallel", "arbitrary")),
        name="hgrn2",
    )(q, q, i, i, a_fwd, a_bwd, lb[0:1], lb[1:2], d_f, d_b, m_f, m_b)


ATTN_TQ = 2048
ATTN_TW = 512


def _qk_prep(x, gain, cos, sin_signed, seg_mean):
    hi, lo = _split2(x * x)
    ms = _dot(hi, seg_mean) + _dot(lo, seg_mean)
    y = x * lax.rsqrt(ms + EPS) * gain
    lane = lax.broadcasted_iota(jnp.int32, y.shape, 1)
    half = QK_DIM // 2
    partner = jnp.where((lane % QK_DIM) < half,
                        pltpu.roll(y, HEAD_W - half, 1), pltpu.roll(y, half, 1))
    return y * cos + partner * sin_signed


ATTN_ONES_ROWS = 16
ATTN_KB = 2048


def _attn_kernel(q_ref, k_ref, v_ref, cosq_ref, sinq_ref, cosk_ref, sink_ref, qg_ref, kg_ref,
                 sg_ref, lam_ref, mean_ref, o_ref, kh_ref, v1t_ref, qp_ref, s0_ref, s1_ref, pa_ref, pb_ref,
                 acc_ref, *, seq, out_scale):
    qi = pl.program_id(2)
    seg_mean = mean_ref[...]
    n_kb = seq // ATTN_KB

    @pl.when(qi == 0)
    def _():
        def prep(r, carry):
            rows = pl.ds(pl.multiple_of(r * ATTN_KB, ATTN_KB), ATTN_KB)
            kh_ref[r] = _qk_prep(k_ref[rows, :], kg_ref[...], cosk_ref[rows, :],
                                 sink_ref[rows, :], seg_mean).astype(BF16)
            v1t_ref[r, 0:HEAD_W, :] = v_ref[rows, :].T.astype(BF16)
            v1t_ref[r, HEAD_W:HEAD_W + ATTN_ONES_ROWS, :] = jnp.ones((ATTN_ONES_ROWS, ATTN_KB), BF16)
            return carry

        lax.fori_loop(0, n_kb, prep, 0)

    tw = ATTN_TW
    n_units = 2 * (q_ref.shape[0] // tw)
    scale = (QK_DIM ** -0.5) * LOG2E
    for t in range(n_units // 2):
        rows = slice(t * tw, (t + 1) * tw)
        qh = _qk_prep(q_ref[rows, :], qg_ref[...], cosq_ref[rows, :], sinq_ref[rows, :], seg_mean) * scale
        lane = lax.broadcasted_iota(jnp.int32, qh.shape, 1)
        qp_ref[2 * t] = jnp.where(lane < QK_DIM, qh, 0.0).astype(BF16)
        qp_ref[2 * t + 1] = jnp.where(lane >= QK_DIM, qh, 0.0).astype(BF16)

    s_refs, p_refs = (s0_ref, s1_ref), (pa_ref, pb_ref)
    neg = jnp.full((8, tw), -jnp.inf, F32)

    def phase(us, ue, m_e):
        def sc(kb, m8):
            if us is None:
                return m8
            s_t = _dot_nt(kh_ref[kb], qp_ref[us])
            s_refs[us % 2][kb] = s_t
            return jnp.maximum(m8, jnp.max(s_t.reshape(ATTN_KB // 8, 8, tw), axis=0))

        def pr(kb, slot):
            if ue is not None:
                p_refs[slot][...] = jnp.exp2(s_refs[ue % 2][kb] - m_e).astype(BF16)

        def va(kb, slot, first=False):
            if ue is not None:
                part = _dot(v1t_ref[kb], p_refs[slot][...])
                acc_ref[ue] = part if first else acc_ref[ue] + part

        m8 = neg
        for kb in range(n_kb):
            m8 = sc(kb, m8)
            pr(kb, kb % 2)
            if kb > 0:
                va(kb - 1, (kb - 1) % 2, first=(kb == 1))
        va(n_kb - 1, (n_kb - 1) % 2, first=(n_kb == 1))
        return jnp.max(m8, axis=0, keepdims=True)

    m = phase(0, None, None)
    for u in range(1, n_units):
        m = phase(u, u - 1, m)
    phase(None, n_units - 1, m)

    for t in range(n_units // 2):
        parts = [acc_ref[2 * t + c, 0:HEAD_W, :] / acc_ref[2 * t + c, HEAD_W:HEAD_W + 1, :]
                 for c in range(2)]
        o = (parts[0] - lam_ref[0, 0] * parts[1]).T
        o_ref[t * tw:(t + 1) * tw, :] = _rms_rows(o, sg_ref[...]) * out_scale


def diff_attention(q, k, v, cos_t, sin_t, qg, kg, sub_g, lam, lambda_init, bsz, seq, tq=ATTN_TQ):
    nq = seq // tq
    n_kb = seq // ATTN_KB
    assert seq % ATTN_KB == 0
    seg = np.kron(np.eye(2, dtype=np.float32), np.full((QK_DIM, QK_DIM), 1.0 / QK_DIM, np.float32))
    qblk = pl.BlockSpec((tq, HEAD_W), lambda b, h, i: (b * nq + i, h))
    kblk = pl.BlockSpec((seq, HEAD_W), lambda b, h, i: (b, h))
    tabq = pl.BlockSpec((tq, HEAD_W), lambda b, h, i: (i, 0))
    tabk = pl.BlockSpec((seq, HEAD_W), lambda b, h, i: (0, 0))
    vec = pl.BlockSpec((1, HEAD_W), lambda b, h, i: (0, 0))
    return pl.pallas_call(
        functools.partial(_attn_kernel, seq=seq, out_scale=1.0 - lambda_init),
        grid=(bsz, N_HEADS, nq),
        in_specs=[qblk, kblk, kblk, tabq, tabq, tabk, tabk, vec, vec, vec,
                  pl.BlockSpec(memory_space=pltpu.SMEM),
                  pl.BlockSpec((HEAD_W, HEAD_W), lambda b, h, i: (0, 0))],
        out_specs=qblk,
        out_shape=jax.ShapeDtypeStruct((bsz * seq, MIX_W), F32),
        scratch_shapes=[pltpu.VMEM((n_kb, ATTN_KB, HEAD_W), BF16),
                        pltpu.VMEM((n_kb, HEAD_W + ATTN_ONES_ROWS, ATTN_KB), BF16),
                        pltpu.VMEM((2 * (tq // ATTN_TW), ATTN_TW, HEAD_W), BF16),
                        pltpu.VMEM((n_kb, ATTN_KB, ATTN_TW), F32),
                        pltpu.VMEM((n_kb, ATTN_KB, ATTN_TW), F32),
                        pltpu.VMEM((ATTN_KB, ATTN_TW), BF16),
                        pltpu.VMEM((ATTN_KB, ATTN_TW), BF16),
                        pltpu.VMEM((2 * (tq // ATTN_TW), HEAD_W + ATTN_ONES_ROWS, ATTN_TW), F32)],
        compiler_params=_cparams(("parallel", "parallel", "arbitrary")),
        name="diff_attention",
    )(q, k, v, cos_t, sin_t, cos_t, sin_t,
      jnp.tile(qg, 2).reshape(1, HEAD_W), jnp.tile(kg, 2).reshape(1, HEAD_W),
      sub_g.reshape(1, HEAD_W), jnp.reshape(lam, (1, 1)).astype(F32), jnp.asarray(seg, BF16))


def _rope_tables(seq):
    inv = 1.0 / (ROPE_THETA ** (jnp.arange(0, QK_DIM, 2, dtype=F32) / QK_DIM))
    ang = jnp.arange(seq, dtype=F32)[:, None] * inv[None, :]
    cos, sin = jnp.cos(ang), jnp.sin(ang)
    return jnp.tile(cos, (1, 4)), jnp.tile(jnp.concatenate([-sin, sin], axis=-1), (1, 2))


def kernel(x, norm_mix_g, norm_mlp_g, w_in_even, w_out_even, sgu_norm_g, sgu_w, sgu_b,
           hgrn_lb_logits, hgrn_norm_g, w_in_odd, w_out_odd, conv_w, q_norm_g, k_norm_g,
           lambda_q1, lambda_k1, lambda_q2, lambda_k2, diff_norm_g, mlp_w1, mlp_w2):
    bsz, seq, d = x.shape
    depth = norm_mix_g.shape[0]
    cos_t, sin_t = _rope_tables(seq)
    p_lb = jax.nn.softmax(hgrn_lb_logits.astype(F32), axis=1)
    lower_bounds = jnp.cumsum(p_lb, axis=1) - p_lb[:, :1]

    w1_bf, w2_bf = mlp_w1.astype(BF16), mlp_w2.astype(BF16)
    wo_even_bf, wo_odd_bf = w_out_even.astype(BF16), w_out_odd.astype(BF16)

    x2 = x.reshape(bsz * seq, d)
    for l in range(depth):
        if l % 2 == 0:
            e = l // 2
            u, v, q, i, g, a_fwd, a_bwd = norm_proj(x2, norm_mix_g[l], w_in_even, e)
            b_full = jnp.broadcast_to(sgu_b[e][:, :, None], (N_HEADS, SGU_CHUNK, HEAD_W))
            o_sum = hgrn2(q, i, a_fwd, a_bwd, lower_bounds[:, e], bsz, seq)
            x2 = mix_mlp(x2, (u, v, o_sum, g), wo_even_bf, e, norm_mlp_g[l], w1_bf, w2_bf, l, seq,
                         sgu_params=(sgu_norm_g[e], sgu_w[e].astype(BF16), b_full, hgrn_norm_g[e]))
        else:
            o = l // 2
            h_in, b_gate, c_gate, q, k, v = norm_proj(x2, norm_mix_g[l], w_in_odd, o)
            lambda_init = 0.8 - 0.6 * math.exp(-0.3 * l)
            lam = (jnp.exp(jnp.sum(lambda_q1[o] * lambda_k1[o]))
                   - jnp.exp(jnp.sum(lambda_q2[o] * lambda_k2[o])) + lambda_init)
            out_d = diff_attention(q, k, v, cos_t, sin_t, q_norm_g[o], k_norm_g[o], diff_norm_g[o],
                                   lam, lambda_init, bsz, seq)
            x2 = mix_mlp(x2, (h_in, b_gate, c_gate, out_d), wo_odd_bf, o, norm_mlp_g[l], w1_bf, w2_bf, l,
                         seq, conv_w=conv_w[o])
    return x2.reshape(bsz, seq, d)
```

```python
import functools
import math

import numpy as np
import jax
import jax.numpy as jnp
from jax import lax
from jax.experimental import pallas as pl
from jax.experimental.pallas import tpu as pltpu

F32 = jnp.float32
BF16 = jnp.bfloat16

EPS = 1e-6
ROPE_THETA = 10000.0
LOG2E = 1.4426950408889634

HEAD_W = 128
N_HEADS = 4
MIX_W = HEAD_W * N_HEADS
SGU_CHUNK = 128
HGRN_CHUNK = 128
QK_DIM = 64

VMEM_LIMIT = 56 * 1024 * 1024


def _cparams(sem):
    return pltpu.CompilerParams(dimension_semantics=sem, vmem_limit_bytes=VMEM_LIMIT)


def _dot(a, b):
    return jnp.dot(a, b, preferred_element_type=F32)


def _dot_nt(a, b):
    return lax.dot_general(a, b, (((1,), (1,)), ((), ())), preferred_element_type=F32)


def _split2(x):
    hi = x.astype(BF16)
    lo = (x - hi.astype(F32)).astype(BF16)
    return hi, lo


def _rms_rows(x, g):
    ms = jnp.mean(x * x, axis=-1, keepdims=True)
    return x * lax.rsqrt(ms + EPS) * g


def _norm_proj_kernel(x_ref, g_ref, w_ref, *rest):
    o_refs, wbf_ref = rest[:-1], rest[-1]

    @pl.when(pl.program_id(0) == 0)
    def _():
        for s in range(len(o_refs)):
            cols = slice(s * MIX_W, (s + 1) * MIX_W)
            wbf_ref[:, cols] = w_ref[:, cols].astype(BF16)

    h = _rms_rows(x_ref[...], g_ref[...]).astype(BF16)
    for s, o_ref in enumerate(o_refs):
        o_ref[...] = _dot(h, wbf_ref[:, s * MIX_W:(s + 1) * MIX_W])


def norm_proj(x2, g, w_stack, layer, tm=512):
    t, d = x2.shape
    n = w_stack.shape[2]
    n_split = n // MIX_W
    return pl.pallas_call(
        _norm_proj_kernel,
        grid=(t // tm,),
        in_specs=[
            pl.BlockSpec((tm, d), lambda i: (i, 0)),
            pl.BlockSpec((1, d), lambda i: (0, 0)),
            pl.BlockSpec((None, d, n), lambda i: (layer, 0, 0), pipeline_mode=pl.Buffered(1)),
        ],
        out_specs=[pl.BlockSpec((tm, MIX_W), lambda i: (i, 0)) for _ in range(n_split)],
        out_shape=[jax.ShapeDtypeStruct((t, MIX_W), F32) for _ in range(n_split)],
        scratch_shapes=[pltpu.VMEM((d, n), BF16)],
        compiler_params=_cparams(("arbitrary",)),
        name="norm_proj",
    )(x2, g.reshape(1, d), w_stack)


def _head_norm_gate(o, g, ng):
    outs = []
    for h in range(N_HEADS):
        cols = slice(h * HEAD_W, (h + 1) * HEAD_W)
        gh = g[:, cols]
        outs.append(_rms_rows(o[:, cols], ng[:, cols]) * (gh * (1.0 / (1.0 + jnp.exp(-gh)))))
    return jnp.concatenate(outs, axis=-1)


def _sgu_block(u, v, g, w_ref, b_ref):
    outs = []
    for h in range(N_HEADS):
        cols = slice(h * HEAD_W, (h + 1) * HEAD_W)
        vn = _rms_rows(jax.nn.gelu(v[:, cols]), g[:, cols])
        mixed = _dot(w_ref[h], vn.astype(BF16)) + b_ref[h]
        outs.append(jax.nn.gelu(u[:, cols]) * mixed)
    return jnp.concatenate(outs, axis=-1)


CONV_HALO = 8


def _mix_mlp_kernel(*refs, even, n_tiles, seq):
    if even:
        (x_ref, u_ref, v_ref, b_ref, gate_ref, sg_ref, ws_ref, bs_ref, ng_ref,
         wo_ref, g_ref, w1_ref, w2_ref, o_ref, h0_ref, h1_ref, acc0_ref, acc1_ref) = refs
    else:
        (x_ref, hin_ref, bg_ref, cg_ref, b_ref, hprev_ref, cprev_ref, hnext_ref, cnext_ref, cw_ref,
         wo_ref, g_ref, w1_ref, w2_ref, o_ref, h0_ref, h1_ref, acc0_ref, acc1_ref) = refs
    h_refs, acc_refs = (h0_ref, h1_ref), (acc0_ref, acc1_ref)
    i, j = pl.program_id(0), pl.program_id(1)
    rb = x_ref.shape[0]
    rows = pl.ds(pl.multiple_of(j * rb, rb), rb)

    def conv_block():
        pos = ((jnp.minimum(i, n_tiles - 1) * pl.num_programs(1) + j) * rb) % seq
        z = cg_ref[...] * hin_ref[...]
        z_before = (cprev_ref[...] * hprev_ref[...])[CONV_HALO - 1:CONV_HALO, :]
        z_after = (cnext_ref[...] * hnext_ref[...])[0:1, :]
        z_before = jnp.where(pos == 0, 0.0, z_before)
        z_after = jnp.where(pos + rb == seq, 0.0, z_after)
        r = lax.broadcasted_iota(jnp.int32, z.shape, 0)
        prev = jnp.where(r == 0, z_before, pltpu.roll(z, 1, 0))
        nxt = jnp.where(r == rb - 1, z_after, pltpu.roll(z, rb - 1, 0))
        y = cw_ref[0:1, :] * prev + cw_ref[1:2, :] * z + cw_ref[2:3, :] * nxt
        return bg_ref[...] * y

    def prepare(slot):
        if even:
            a = jnp.concatenate(
                [_sgu_block(u_ref[r:r + SGU_CHUNK, :], v_ref[r:r + SGU_CHUNK, :], sg_ref[...], ws_ref, bs_ref)
                 for r in range(0, rb, SGU_CHUNK)], axis=0)
            b = _head_norm_gate(b_ref[...], gate_ref[...], ng_ref[...])
        else:
            a, b = conv_block(), b_ref[...]
        x1 = x_ref[...] + _dot(a.astype(BF16), wo_ref[:MIX_W, :])
        x1 += _dot(b.astype(BF16), wo_ref[MIX_W:, :])
        acc_refs[slot][rows, :] = x1
        h_refs[slot][rows, :] = _rms_rows(x1, g_ref[...]).astype(BF16)

    def mlp_chunk(slot):
        hm = _dot(h_refs[slot][...], w1_ref[...])
        act = jnp.square(jnp.maximum(hm, 0.0)).astype(BF16)
        acc_refs[slot][...] += _dot(act, w2_ref[...])

    odd_step = (i % 2) == 1

    @pl.when(i == 0)
    def _():
        prepare(0)

    @pl.when((i > 0) & (i < n_tiles) & jnp.logical_not(odd_step))
    def _():
        mlp_chunk(1)
        prepare(0)

    @pl.when(odd_step)
    def _():
        mlp_chunk(0)
        prepare(1)

    @pl.when(i == n_tiles)
    def _():
        mlp_chunk(1)

    last = j == pl.num_programs(1) - 1

    @pl.when(last & odd_step)
    def _():
        o_ref[...] = acc0_ref[...]

    @pl.when(last & (i > 0) & jnp.logical_not(odd_step))
    def _():
        o_ref[...] = acc1_ref[...]


def mix_mlp(x2, mix_in, wo_bf, wo_layer, g, w1_bf, w2_bf, layer, seq, sgu_params=None, conv_w=None,
            tm=1024, tf=1024):
    t, d = x2.shape
    ff = w1_bf.shape[2]
    n_tiles, n_j = t // tm, ff // tf
    rb = tm // n_j
    assert n_tiles % 2 == 0 and rb % SGU_CHUNK == 0 and seq % rb == 0
    even = sgu_params is not None

    def row_block(i, j):
        return jnp.minimum(i, n_tiles - 1) * n_j + j

    def blk(width):
        return pl.BlockSpec((rb, width), lambda i, j: (row_block(i, j), 0))

    def const(shape):
        return pl.BlockSpec(shape, lambda i, j: (0,) * len(shape))

    ins = [x2] + list(mix_in)
    specs = [blk(d)] + [blk(MIX_W) for _ in mix_in]
    if even:
        norm_g, w_s_bf, b_full, hgrn_norm_g = sgu_params
        ins += [norm_g.reshape(1, MIX_W), w_s_bf, b_full, hgrn_norm_g.reshape(1, MIX_W)]
        specs += [const((1, MIX_W)), const((N_HEADS, SGU_CHUNK, SGU_CHUNK)),
                  const((N_HEADS, SGU_CHUNK, HEAD_W)), const((1, MIX_W))]
    else:
        h_in, _, c_gate, _ = mix_in
        per = rb // CONV_HALO
        before = pl.BlockSpec((CONV_HALO, MIX_W),
                              lambda i, j: (jnp.maximum(row_block(i, j) * per - 1, 0), 0))
        after = pl.BlockSpec((CONV_HALO, MIX_W),
                             lambda i, j: (jnp.minimum((row_block(i, j) + 1) * per, t // CONV_HALO - 1), 0))
        ins += [h_in, c_gate, h_in, c_gate, conv_w]
        specs += [before, before, after, after, const((3, MIX_W))]
    return pl.pallas_call(
        functools.partial(_mix_mlp_kernel, even=even, n_tiles=n_tiles, seq=seq),
        grid=(n_tiles + 1, n_j),
        in_specs=specs + [
            pl.BlockSpec((None, 2 * MIX_W, d), lambda i, j: (wo_layer, 0, 0)),
            const((1, d)),
            pl.BlockSpec((None, d, tf), lambda i, j: (layer, 0, j)),
            pl.BlockSpec((None, tf, d), lambda i, j: (layer, j, 0)),
        ],
        out_specs=pl.BlockSpec((tm, d), lambda i, j: (jnp.maximum(i - 1, 0), 0)),
        out_shape=jax.ShapeDtypeStruct((t, d), F32),
        scratch_shapes=[pltpu.VMEM((tm, d), BF16), pltpu.VMEM((tm, d), BF16),
                        pltpu.VMEM((tm, d), F32), pltpu.VMEM((tm, d), F32)],
        compiler_params=_cparams(("arbitrary", "arbitrary")),
        name="mix_mlp",
    )(*ins, wo_bf, g.reshape(1, d), w1_bf, w2_bf)


SUBLANES = 8


def _hgrn_tables(c):
    levels = int(math.log2(c))
    assert 1 << levels == c
    pos = np.arange(c)
    t = pos[:, None]
    i = pos[None, :]
    d_rows, m_rows = [], []
    for lv in range(levels):
        h = c >> (lv + 1)
        start = (t // (2 * h)) * (2 * h)
        mid = start + h - 1
        upper = (t - start) >= h
        if h < SUBLANES:
            d_rows.append(np.where(upper, (i > mid) & (i <= t), (i > t) & (i <= mid)))
        same = (t // (2 * h)) == (i // (2 * h))
        m_rows.append(same & upper & ((i % (2 * h)) < h))
    n_fine = len(d_rows)
    d_rows.append(i <= t)
    d = np.concatenate(d_rows, axis=0).astype(np.float32)
    m = np.concatenate(m_rows, axis=0).astype(np.float32)
    return d, m, levels, n_fine


def _flip_blocks(a, c):
    blocks = a.reshape(-1, c, c)
    return blocks[:, ::-1, ::-1].reshape(a.shape)


def _coarse_level(qh, kh, inc, h, fwd):
    c = inc.shape[0]
    zeros = jnp.zeros((h, inc.shape[1]), F32)
    q_rows, k_rows = [], []
    for n in range(c // (2 * h)):
        lo = slice(n * 2 * h, n * 2 * h + h)
        up = slice(n * 2 * h + h, (n + 1) * 2 * h)
        if fwd:
            ref = inc[n * 2 * h + h - 1:n * 2 * h + h, :]
            q_rows += [zeros, qh[up] * jnp.exp2(inc[up] - ref)]
            k_rows += [kh[lo] * jnp.exp2(ref - inc[lo]), zeros]
        else:
            ref = inc[n * 2 * h + h:n * 2 * h + h + 1, :]
            q_rows += [qh[lo] * jnp.exp2(inc[lo] - ref), zeros]
            k_rows += [zeros, kh[up] * jnp.exp2(ref - inc[up])]
    return jnp.concatenate(q_rows, axis=0), jnp.concatenate(k_rows, axis=0)


def _hgrn_direction(q, v_bf, vt_bf, a, lb, d_ref, m_ref, st_ref, *, levels, n_fine, fwd):
    c = q.shape[0]
    f = lb + (1.0 - lb) * (1.0 / (1.0 + jnp.exp(-a)))
    lf = jnp.log(f) * LOG2E
    kk = 1.0 - f
    hi, lo = _split2(lf)
    sums = _dot(d_ref[...], hi) + _dot(d_ref[...], lo)
    e_all = jnp.exp2(sums)
    inc_rows = slice(n_fine * c, (n_fine + 1) * c)
    last_row = c - 1 if fwd else 0
    row = lax.broadcasted_iota(jnp.int32, (c, c), 0)
    col = lax.broadcasted_iota(jnp.int32, (c, c), 1)
    eye = row == col
    outs = []
    for h in range(N_HEADS):
        cols = slice(h * HEAD_W, (h + 1) * HEAD_W)
        qh, kh, inc = q[:, cols], kk[:, cols], sums[inc_rows, cols]
        att = jnp.where(eye, jnp.sum(qh * kh, axis=-1, keepdims=True), 0.0)
        for lv in range(levels):
            half = c >> (lv + 1)
            if half >= SUBLANES:
                ql, kl = _coarse_level(qh, kh, inc, half, fwd)
            else:
                fine = lv - (levels - n_fine)
                e_lv = e_all[fine * c:(fine + 1) * c, cols]
                ql, kl = qh * e_lv, kh * e_lv
            pair = _dot_nt(ql.astype(BF16), kl.astype(BF16))
            att += pair if lv == 0 else pair * m_ref[lv * c:(lv + 1) * c, :]
        e_inc = e_all[inc_rows, cols]
        st = st_ref[h]
        o_h = _dot(att.astype(BF16), v_bf[:, cols])
        o_h += _dot_nt((qh * e_inc).astype(BF16), st.astype(BF16))
        k_dec = (kh * jnp.exp2(inc[last_row:last_row + 1, :] - inc)).astype(BF16)
        st_ref[h] = st * e_inc[last_row:last_row + 1, :] + _dot(vt_bf[cols, :], k_dec)
        outs.append(o_h)
    return jnp.concatenate(outs, axis=-1)


HGRN_STEP_CHUNKS = 8


def _hgrn_kernel(qf_ref, qb_ref, if_ref, ib_ref, af_ref, ab_ref, lbf_ref, lbb_ref,
                 df_ref, db_ref, mf_ref, mb_ref, o_ref, stf_ref, stb_ref, *, levels, n_fine, chunk):
    n = pl.program_id(1)
    n_steps = pl.num_programs(1)
    blk = HGRN_STEP_CHUNKS * chunk

    @pl.when(n == 0)
    def _():
        stf_ref[...] = jnp.zeros_like(stf_ref)
        stb_ref[...] = jnp.zeros_like(stb_ref)

    def run(q_ref, i_ref, a_ref, lb_ref, d_ref, m_ref, st_ref, sub, fwd):
        rows = slice(sub * chunk, (sub + 1) * chunk)
        v = i_ref[rows, :]
        return _hgrn_direction(q_ref[rows, :], v.astype(BF16), v.T.astype(BF16), a_ref[rows, :],
                               lb_ref[...], d_ref, m_ref, st_ref, levels=levels, n_fine=n_fine, fwd=fwd)

    o_f, o_b = [], [None] * HGRN_STEP_CHUNKS
    for sub in range(HGRN_STEP_CHUNKS):
        rev = HGRN_STEP_CHUNKS - 1 - sub
        o_f.append(run(qf_ref, if_ref, af_ref, lbf_ref, df_ref, mf_ref, stf_ref, sub, True))
        o_b[rev] = run(qb_ref, ib_ref, ab_ref, lbb_ref, db_ref, mb_ref, stb_ref, rev, False)
    o_f = jnp.concatenate(o_f, axis=0)
    o_b = jnp.concatenate(o_b, axis=0)
    rows_f = pl.ds(pl.multiple_of(n * blk, blk), blk)
    rows_b = pl.ds(pl.multiple_of((n_steps - 1 - n) * blk, blk), blk)

    @pl.when(2 * n < n_steps)
    def _():
        o_ref[rows_f, :] = o_f
        o_ref[rows_b, :] = o_b

    @pl.when(2 * n >= n_steps)
    def _():
        o_ref[rows_f, :] += o_f
        o_ref[rows_b, :] += o_b


def hgrn2(q, i, a_fwd, a_bwd, lb, bsz, seq, chunk=HGRN_CHUNK):
    blk = HGRN_STEP_CHUNKS * chunk
    n_steps = seq // blk
    assert n_steps % 2 == 0
    d_np, m_np, levels, n_fine = _hgrn_tables(chunk)
    d_f = jnp.asarray(d_np, BF16)
    d_b = jnp.asarray(_flip_blocks(d_np, chunk), BF16)
    m_f = jnp.asarray(m_np, F32)
    m_b = jnp.asarray(_flip_blocks(m_np, chunk), F32)

    fwd = pl.BlockSpec((blk, MIX_W), lambda b, n: (b * n_steps + n, 0))
    bwd = pl.BlockSpec((blk, MIX_W), lambda b, n: (b * n_steps + n_steps - 1 - n, 0))
    vec = pl.BlockSpec((1, MIX_W), lambda b, n: (0, 0))

    def table(a):
        return pl.BlockSpec(a.shape, lambda b, n: (0, 0))

    return pl.pallas_call(
        functools.partial(_hgrn_kernel, levels=levels, n_fine=n_fine, chunk=chunk),
        grid=(bsz, n_steps),
        in_specs=[fwd, bwd, fwd, bwd, fwd, bwd, vec, vec,
                  table(d_f), table(d_b), table(m_f), table(m_b)],
        out_specs=pl.BlockSpec((seq, MIX_W), lambda b, n: (b, 0)),
        out_shape=jax.ShapeDtypeStruct((bsz * seq, MIX_W), F32),
        scratch_shapes=[pltpu.VMEM((N_HEADS, HEAD_W, HEAD_W), F32),
                        pltpu.VMEM((N_HEADS, HEAD_W, HEAD_W), F32)],
        compiler_params=_cparams(("parallel", "arbitrary")),
        name="hgrn2",
    )(q, q, i, i, a_fwd, a_bwd, lb[0:1], lb[1:2], d_f, d_b, m_f, m_b)


ATTN_TQ = 4096
ATTN_TW = 512


def _qk_prep(x, gain, cos, sin_signed, seg_mean):
    hi, lo = _split2(x * x)
    ms = _dot(hi, seg_mean) + _dot(lo, seg_mean)
    y = x * lax.rsqrt(ms + EPS) * gain
    lane = lax.broadcasted_iota(jnp.int32, y.shape, 1)
    half = QK_DIM // 2
    partner = jnp.where((lane % QK_DIM) < half,
                        pltpu.roll(y, HEAD_W - half, 1), pltpu.roll(y, half, 1))
    return y * cos + partner * sin_signed


ATTN_ONES_ROWS = 16
ATTN_KB = 2048


def _attn_kernel(q_ref, k_ref, v_ref, cosq_ref, sinq_ref, cosk_ref, sink_ref, qg_ref, kg_ref,
                 sg_ref, lam_ref, mean_ref, o_ref, kh_ref, v1t_ref, qp_ref, s0_ref, s1_ref, pa_ref, pb_ref,
                 acc_ref, *, seq):
    qi = pl.program_id(2)
    seg_mean = mean_ref[...]
    n_kb = seq // ATTN_KB

    @pl.when(qi == 0)
    def _():
        def prep(r, carry):
            rows = pl.ds(pl.multiple_of(r * ATTN_KB, ATTN_KB), ATTN_KB)
            kh_ref[r] = _qk_prep(k_ref[rows, :], kg_ref[...], cosk_ref[rows, :],
                                 sink_ref[rows, :], seg_mean).astype(BF16)
            v1t_ref[r, 0:HEAD_W, :] = v_ref[rows, :].T.astype(BF16)
            v1t_ref[r, HEAD_W:HEAD_W + ATTN_ONES_ROWS, :] = jnp.ones((ATTN_ONES_ROWS, ATTN_KB), BF16)
            return carry

        lax.fori_loop(0, n_kb, prep, 0)

    tw = ATTN_TW
    n_units = 2 * (q_ref.shape[0] // tw)
    scale = (QK_DIM ** -0.5) * LOG2E
    for t in range(n_units // 2):
        rows = slice(t * tw, (t + 1) * tw)
        qh = _qk_prep(q_ref[rows, :], qg_ref[...], cosq_ref[rows, :], sinq_ref[rows, :], seg_mean) * scale
        lane = lax.broadcasted_iota(jnp.int32, qh.shape, 1)
        qp_ref[2 * t] = jnp.where(lane < QK_DIM, qh, 0.0).astype(BF16)
        qp_ref[2 * t + 1] = jnp.where(lane >= QK_DIM, qh, 0.0).astype(BF16)

    s_refs, p_refs = (s0_ref, s1_ref), (pa_ref, pb_ref)
    neg = jnp.full((8, tw), -jnp.inf, F32)

    def phase(us, ue, m_e):
        def sc(kb, m8):
            if us is None:
                return m8
            s_t = _dot_nt(kh_ref[kb], qp_ref[us])
            s_refs[us % 2][kb] = s_t
            return jnp.maximum(m8, jnp.max(s_t.reshape(ATTN_KB // 8, 8, tw), axis=0))

        def pr(kb, slot):
            if ue is not None:
                p_refs[slot][...] = jnp.exp2(s_refs[ue % 2][kb] - m_e).astype(BF16)

        def va(kb, slot, first=False):
            if ue is not None:
                part = _dot(v1t_ref[kb], p_refs[slot][...])
                acc_ref[ue] = part if first else acc_ref[ue] + part

        m8 = neg
        for kb in range(n_kb):
            m8 = sc(kb, m8)
            pr(kb, kb % 2)
            if kb > 0:
                va(kb - 1, (kb - 1) % 2, first=(kb == 1))
        va(n_kb - 1, (n_kb - 1) % 2, first=(n_kb == 1))
        return jnp.max(m8, axis=0, keepdims=True)

    m = phase(0, None, None)
    for u in range(1, n_units):
        m = phase(u, u - 1, m)
    phase(None, n_units - 1, m)

    for t in range(n_units // 2):
        parts = [acc_ref[2 * t + c, 0:HEAD_W, :] / acc_ref[2 * t + c, HEAD_W:HEAD_W + 1, :]
                 for c in range(2)]
        o = (parts[0] - lam_ref[0, 0] * parts[1]).T
        o_ref[t * tw:(t + 1) * tw, :] = _rms_rows(o, sg_ref[...]) * lam_ref[0, 1]


def diff_attention(q, k, v, cos_t, sin_t, qg, kg, sub_g, lam, lambda_init, bsz, seq, tq=ATTN_TQ):
    nq = seq // tq
    n_kb = seq // ATTN_KB
    assert seq % ATTN_KB == 0
    seg = np.kron(np.eye(2, dtype=np.float32), np.full((QK_DIM, QK_DIM), 1.0 / QK_DIM, np.float32))
    qblk = pl.BlockSpec((tq, HEAD_W), lambda b, h, i: (b * nq + i, h))
    kblk = pl.BlockSpec((seq, HEAD_W), lambda b, h, i: (b, h))
    tabq = pl.BlockSpec((tq, HEAD_W), lambda b, h, i: (i, 0))
    tabk = pl.BlockSpec((seq, HEAD_W), lambda b, h, i: (0, 0))
    vec = pl.BlockSpec((1, HEAD_W), lambda b, h, i: (0, 0))
    return pl.pallas_call(
        functools.partial(_attn_kernel, seq=seq),
        grid=(bsz, N_HEADS, nq),
        in_specs=[qblk, kblk, kblk, tabq, tabq, tabk, tabk, vec, vec, vec,
                  pl.BlockSpec(memory_space=pltpu.SMEM),
                  pl.BlockSpec((HEAD_W, HEAD_W), lambda b, h, i: (0, 0))],
        out_specs=qblk,
        out_shape=jax.ShapeDtypeStruct((bsz * seq, MIX_W), F32),
        scratch_shapes=[pltpu.VMEM((n_kb, ATTN_KB, HEAD_W), BF16),
                        pltpu.VMEM((n_kb, HEAD_W + ATTN_ONES_ROWS, ATTN_KB), BF16),
                        pltpu.VMEM((2 * (tq // ATTN_TW), ATTN_TW, HEAD_W), BF16),
                        pltpu.VMEM((n_kb, ATTN_KB, ATTN_TW), F32),
                        pltpu.VMEM((n_kb, ATTN_KB, ATTN_TW), F32),
                        pltpu.VMEM((ATTN_KB, ATTN_TW), BF16),
                        pltpu.VMEM((ATTN_KB, ATTN_TW), BF16),
                        pltpu.VMEM((2 * (tq // ATTN_TW), HEAD_W + ATTN_ONES_ROWS, ATTN_TW), F32)],
        compiler_params=_cparams(("parallel", "parallel", "arbitrary")),
        name="diff_attention",
    )(q, k, v, cos_t, sin_t, cos_t, sin_t,
      jnp.tile(qg, 2).reshape(1, HEAD_W), jnp.tile(kg, 2).reshape(1, HEAD_W),
      sub_g.reshape(1, HEAD_W),
      jnp.stack([lam, jnp.asarray(1.0 - lambda_init, F32)]).reshape(1, 2).astype(F32),
      jnp.asarray(seg, BF16))


def _rope_tables(seq):
    inv = 1.0 / (ROPE_THETA ** (jnp.arange(0, QK_DIM, 2, dtype=F32) / QK_DIM))
    ang = jnp.arange(seq, dtype=F32)[:, None] * inv[None, :]
    cos, sin = jnp.cos(ang), jnp.sin(ang)
    return jnp.tile(cos, (1, 4)), jnp.tile(jnp.concatenate([-sin, sin], axis=-1), (1, 2))


def kernel(x, norm_mix_g, norm_mlp_g, w_in_even, w_out_even, sgu_norm_g, sgu_w, sgu_b,
           hgrn_lb_logits, hgrn_norm_g, w_in_odd, w_out_odd, conv_w, q_norm_g, k_norm_g,
           lambda_q1, lambda_k1, lambda_q2, lambda_k2, diff_norm_g, mlp_w1, mlp_w2):
    bsz, seq, d = x.shape
    depth = norm_mix_g.shape[0]
    cos_t, sin_t = _rope_tables(seq)
    p_lb = jax.nn.softmax(hgrn_lb_logits.astype(F32), axis=1)
    lower_bounds = jnp.cumsum(p_lb, axis=1) - p_lb[:, :1]

    w1_bf, w2_bf = mlp_w1.astype(BF16), mlp_w2.astype(BF16)
    wo_even_bf, wo_odd_bf = w_out_even.astype(BF16), w_out_odd.astype(BF16)

    x2 = x.reshape(bsz * seq, d)
    for l in range(depth):
        if l % 2 == 0:
            e = l // 2
            u, v, q, i, g, a_fwd, a_bwd = norm_proj(x2, norm_mix_g[l], w_in_even, e)
            b_full = jnp.broadcast_to(sgu_b[e][:, :, None], (N_HEADS, SGU_CHUNK, HEAD_W))
            o_sum = hgrn2(q, i, a_fwd, a_bwd, lower_bounds[:, e], bsz, seq)
            x2 = mix_mlp(x2, (u, v, o_sum, g), wo_even_bf, e, norm_mlp_g[l], w1_bf, w2_bf, l, seq,
                         sgu_params=(sgu_norm_g[e], sgu_w[e].astype(BF16), b_full, hgrn_norm_g[e]))
        else:
            o = l // 2
            h_in, b_gate, c_gate, q, k, v = norm_proj(x2, norm_mix_g[l], w_in_odd, o)
            lambda_init = 0.8 - 0.6 * math.exp(-0.3 * l)
            lam = (jnp.exp(jnp.sum(lambda_q1[o] * lambda_k1[o]))
                   - jnp.exp(jnp.sum(lambda_q2[o] * lambda_k2[o])) + lambda_init)
            out_d = diff_attention(q, k, v, cos_t, sin_t, q_norm_g[o], k_norm_g[o], diff_norm_g[o],
                                   lam, lambda_init, bsz, seq)
            x2 = mix_mlp(x2, (h_in, b_gate, c_gate, out_d), wo_odd_bf, o, norm_mlp_g[l], w1_bf, w2_bf, l,
                         seq, conv_w=conv_w[o])
    return x2.reshape(bsz, seq, d)
```

```python
import functools
import math

import numpy as np
import jax
import jax.numpy as jnp
from jax import lax
from jax.experimental import pallas as pl
from jax.experimental.pallas import tpu as pltpu

F32 = jnp.float32
BF16 = jnp.bfloat16

EPS = 1e-6
ROPE_THETA = 10000.0
LOG2E = 1.4426950408889634

HEAD_W = 128
N_HEADS = 4
MIX_W = HEAD_W * N_HEADS
SGU_CHUNK = 128
HGRN_CHUNK = 64
QK_DIM = 64

VMEM_LIMIT = 56 * 1024 * 1024


def _cparams(sem):
    return pltpu.CompilerParams(dimension_semantics=sem, vmem_limit_bytes=VMEM_LIMIT)


def _dot(a, b):
    return jnp.dot(a, b, preferred_element_type=F32)


def _dot_nt(a, b):
    return lax.dot_general(a, b, (((1,), (1,)), ((), ())), preferred_element_type=F32)


def _split2(x):
    hi = x.astype(BF16)
    lo = (x - hi.astype(F32)).astype(BF16)
    return hi, lo


def _rms_rows(x, g):
    ms = jnp.mean(x * x, axis=-1, keepdims=True)
    return x * lax.rsqrt(ms + EPS) * g


def _norm_proj_kernel(x_ref, g_ref, w_ref, *rest):
    o_refs, wbf_ref = rest[:-1], rest[-1]

    @pl.when(pl.program_id(0) == 0)
    def _():
        for s in range(len(o_refs)):
            cols = slice(s * MIX_W, (s + 1) * MIX_W)
            wbf_ref[:, cols] = w_ref[:, cols].astype(BF16)

    h = _rms_rows(x_ref[...], g_ref[...]).astype(BF16)
    for s, o_ref in enumerate(o_refs):
        o_ref[...] = _dot(h, wbf_ref[:, s * MIX_W:(s + 1) * MIX_W])


def norm_proj(x2, g, w_stack, layer, tm=512):
    t, d = x2.shape
    n = w_stack.shape[2]
    n_split = n // MIX_W
    return pl.pallas_call(
        _norm_proj_kernel,
        grid=(t // tm,),
        in_specs=[
            pl.BlockSpec((tm, d), lambda i: (i, 0)),
            pl.BlockSpec((1, d), lambda i: (0, 0)),
            pl.BlockSpec((None, d, n), lambda i: (layer, 0, 0), pipeline_mode=pl.Buffered(1)),
        ],
        out_specs=[pl.BlockSpec((tm, MIX_W), lambda i: (i, 0)) for _ in range(n_split)],
        out_shape=[jax.ShapeDtypeStruct((t, MIX_W), F32) for _ in range(n_split)],
        scratch_shapes=[pltpu.VMEM((d, n), BF16)],
        compiler_params=_cparams(("arbitrary",)),
        name="norm_proj",
    )(x2, g.reshape(1, d), w_stack)


def _head_norm_gate(o, g, ng):
    outs = []
    for h in range(N_HEADS):
        cols = slice(h * HEAD_W, (h + 1) * HEAD_W)
        gh = g[:, cols]
        outs.append(_rms_rows(o[:, cols], ng[:, cols]) * (gh * (1.0 / (1.0 + jnp.exp(-gh)))))
    return jnp.concatenate(outs, axis=-1)


def _sgu_block(u, v, g, w_ref, b_ref):
    outs = []
    for h in range(N_HEADS):
        cols = slice(h * HEAD_W, (h + 1) * HEAD_W)
        vn = _rms_rows(jax.nn.gelu(v[:, cols]), g[:, cols])
        mixed = _dot(w_ref[h], vn.astype(BF16)) + b_ref[h]
        outs.append(jax.nn.gelu(u[:, cols]) * mixed)
    return jnp.concatenate(outs, axis=-1)


CONV_HALO = 8


def _mix_mlp_kernel(*refs, even, n_tiles, seq):
    if even:
        (x_ref, u_ref, v_ref, b_ref, gate_ref, sg_ref, ws_ref, bs_ref, ng_ref,
         wo_ref, g_ref, w1_ref, w2_ref, o_ref, h0_ref, h1_ref, acc0_ref, acc1_ref) = refs
    else:
        (x_ref, hin_ref, bg_ref, cg_ref, b_ref, hprev_ref, cprev_ref, hnext_ref, cnext_ref, cw_ref,
         wo_ref, g_ref, w1_ref, w2_ref, o_ref, h0_ref, h1_ref, acc0_ref, acc1_ref) = refs
    h_refs, acc_refs = (h0_ref, h1_ref), (acc0_ref, acc1_ref)
    i, j = pl.program_id(0), pl.program_id(1)
    rb = x_ref.shape[0]
    rows = pl.ds(pl.multiple_of(j * rb, rb), rb)

    def conv_block():
        pos = ((jnp.minimum(i, n_tiles - 1) * pl.num_programs(1) + j) * rb) % seq
        z = cg_ref[...] * hin_ref[...]
        z_before = (cprev_ref[...] * hprev_ref[...])[CONV_HALO - 1:CONV_HALO, :]
        z_after = (cnext_ref[...] * hnext_ref[...])[0:1, :]
        z_before = jnp.where(pos == 0, 0.0, z_before)
        z_after = jnp.where(pos + rb == seq, 0.0, z_after)
        r = lax.broadcasted_iota(jnp.int32, z.shape, 0)
        prev = jnp.where(r == 0, z_before, pltpu.roll(z, 1, 0))
        nxt = jnp.where(r == rb - 1, z_after, pltpu.roll(z, rb - 1, 0))
        y = cw_ref[0:1, :] * prev + cw_ref[1:2, :] * z + cw_ref[2:3, :] * nxt
        return bg_ref[...] * y

    def prepare(slot):
        if even:
            a = jnp.concatenate(
                [_sgu_block(u_ref[r:r + SGU_CHUNK, :], v_ref[r:r + SGU_CHUNK, :], sg_ref[...], ws_ref, bs_ref)
                 for r in range(0, rb, SGU_CHUNK)], axis=0)
            b = _head_norm_gate(b_ref[...], gate_ref[...], ng_ref[...])
        else:
            a, b = conv_block(), b_ref[...]
        x1 = x_ref[...] + _dot(a.astype(BF16), wo_ref[:MIX_W, :])
        x1 += _dot(b.astype(BF16), wo_ref[MIX_W:, :])
        acc_refs[slot][rows, :] = x1
        h_refs[slot][rows, :] = _rms_rows(x1, g_ref[...]).astype(BF16)

    def mlp_chunk(slot):
        hm = _dot(h_refs[slot][...], w1_ref[...])
        act = jnp.square(jnp.maximum(hm, 0.0)).astype(BF16)
        acc_refs[slot][...] += _dot(act, w2_ref[...])

    odd_step = (i % 2) == 1

    @pl.when(i == 0)
    def _():
        prepare(0)

    @pl.when((i > 0) & (i < n_tiles) & jnp.logical_not(odd_step))
    def _():
        mlp_chunk(1)
        prepare(0)

    @pl.when(odd_step)
    def _():
        mlp_chunk(0)
        prepare(1)

    @pl.when(i == n_tiles)
    def _():
        mlp_chunk(1)

    last = j == pl.num_programs(1) - 1

    @pl.when(last & odd_step)
    def _():
        o_ref[...] = acc0_ref[...]

    @pl.when(last & (i > 0) & jnp.logical_not(odd_step))
    def _():
        o_ref[...] = acc1_ref[...]


def mix_mlp(x2, mix_in, wo_bf, wo_layer, g, w1_bf, w2_bf, layer, seq, sgu_params=None, conv_w=None,
            tm=1024, tf=1024):
    t, d = x2.shape
    ff = w1_bf.shape[2]
    n_tiles, n_j = t // tm, ff // tf
    rb = tm // n_j
    assert n_tiles % 2 == 0 and rb % SGU_CHUNK == 0 and seq % rb == 0
    even = sgu_params is not None

    def row_block(i, j):
        return jnp.minimum(i, n_tiles - 1) * n_j + j

    def blk(width):
        return pl.BlockSpec((rb, width), lambda i, j: (row_block(i, j), 0))

    def const(shape):
        return pl.BlockSpec(shape, lambda i, j: (0,) * len(shape))

    ins = [x2] + list(mix_in)
    specs = [blk(d)] + [blk(MIX_W) for _ in mix_in]
    if even:
        norm_g, w_s_bf, b_full, hgrn_norm_g = sgu_params
        ins += [norm_g.reshape(1, MIX_W), w_s_bf, b_full, hgrn_norm_g.reshape(1, MIX_W)]
        specs += [const((1, MIX_W)), const((N_HEADS, SGU_CHUNK, SGU_CHUNK)),
                  const((N_HEADS, SGU_CHUNK, HEAD_W)), const((1, MIX_W))]
    else:
        h_in, _, c_gate, _ = mix_in
        per = rb // CONV_HALO
        before = pl.BlockSpec((CONV_HALO, MIX_W),
                              lambda i, j: (jnp.maximum(row_block(i, j) * per - 1, 0), 0))
        after = pl.BlockSpec((CONV_HALO, MIX_W),
                             lambda i, j: (jnp.minimum((row_block(i, j) + 1) * per, t // CONV_HALO - 1), 0))
        ins += [h_in, c_gate, h_in, c_gate, conv_w]
        specs += [before, before, after, after, const((3, MIX_W))]
    return pl.pallas_call(
        functools.partial(_mix_mlp_kernel, even=even, n_tiles=n_tiles, seq=seq),
        grid=(n_tiles + 1, n_j),
        in_specs=specs + [
            pl.BlockSpec((None, 2 * MIX_W, d), lambda i, j: (wo_layer, 0, 0)),
            const((1, d)),
            pl.BlockSpec((None, d, tf), lambda i, j: (layer, 0, j)),
            pl.BlockSpec((None, tf, d), lambda i, j: (layer, j, 0)),
        ],
        out_specs=pl.BlockSpec((tm, d), lambda i, j: (jnp.maximum(i - 1, 0), 0)),
        out_shape=jax.ShapeDtypeStruct((t, d), F32),
        scratch_shapes=[pltpu.VMEM((tm, d), BF16), pltpu.VMEM((tm, d), BF16),
                        pltpu.VMEM((tm, d), F32), pltpu.VMEM((tm, d), F32)],
        compiler_params=_cparams(("arbitrary", "arbitrary")),
        name="mix_mlp",
    )(*ins, wo_bf, g.reshape(1, d), w1_bf, w2_bf)


SUBLANES = 8


def _hgrn_tables(c):
    levels = int(math.log2(c))
    assert 1 << levels == c
    pos = np.arange(c)
    t = pos[:, None]
    i = pos[None, :]
    d_rows, m_rows = [], []
    for lv in range(levels):
        h = c >> (lv + 1)
        start = (t // (2 * h)) * (2 * h)
        mid = start + h - 1
        upper = (t - start) >= h
        if h < SUBLANES:
            d_rows.append(np.where(upper, (i > mid) & (i <= t), (i > t) & (i <= mid)))
        same = (t // (2 * h)) == (i // (2 * h))
        m_rows.append(same & upper & ((i % (2 * h)) < h))
    n_fine = len(d_rows)
    d_rows.append(i <= t)
    d = np.concatenate(d_rows, axis=0).astype(np.float32)
    m = np.concatenate(m_rows, axis=0).astype(np.float32)
    return d, m, levels, n_fine


def _flip_blocks(a, c):
    blocks = a.reshape(-1, c, c)
    return blocks[:, ::-1, ::-1].reshape(a.shape)


def _coarse_level(qh, kh, inc, h, fwd):
    c = inc.shape[0]
    zeros = jnp.zeros((h, inc.shape[1]), F32)
    q_rows, k_rows = [], []
    for n in range(c // (2 * h)):
        lo = slice(n * 2 * h, n * 2 * h + h)
        up = slice(n * 2 * h + h, (n + 1) * 2 * h)
        if fwd:
            ref = inc[n * 2 * h + h - 1:n * 2 * h + h, :]
            q_rows += [zeros, qh[up] * jnp.exp2(inc[up] - ref)]
            k_rows += [kh[lo] * jnp.exp2(ref - inc[lo]), zeros]
        else:
            ref = inc[n * 2 * h + h:n * 2 * h + h + 1, :]
            q_rows += [qh[lo] * jnp.exp2(inc[lo] - ref), zeros]
            k_rows += [zeros, kh[up] * jnp.exp2(ref - inc[up])]
    return jnp.concatenate(q_rows, axis=0), jnp.concatenate(k_rows, axis=0)


def _hgrn_direction(q, v_bf, vt_bf, a, lb, d_ref, m_ref, st_ref, *, levels, n_fine, fwd):
    c = q.shape[0]
    f = lb + (1.0 - lb) * (1.0 / (1.0 + jnp.exp(-a)))
    lf = jnp.log(f) * LOG2E
    kk = 1.0 - f
    hi, lo = _split2(lf)
    sums = _dot(d_ref[...], hi) + _dot(d_ref[...], lo)
    e_all = jnp.exp2(sums)
    inc_rows = slice(n_fine * c, (n_fine + 1) * c)
    last_row = c - 1 if fwd else 0
    row = lax.broadcasted_iota(jnp.int32, (c, c), 0)
    col = lax.broadcasted_iota(jnp.int32, (c, c), 1)
    eye = row == col
    outs = []
    for h in range(N_HEADS):
        cols = slice(h * HEAD_W, (h + 1) * HEAD_W)
        qh, kh, inc = q[:, cols], kk[:, cols], sums[inc_rows, cols]
        att = jnp.where(eye, jnp.sum(qh * kh, axis=-1, keepdims=True), 0.0)
        for lv in range(levels):
            half = c >> (lv + 1)
            if half >= SUBLANES:
                ql, kl = _coarse_level(qh, kh, inc, half, fwd)
            else:
                fine = lv - (levels - n_fine)
                e_lv = e_all[fine * c:(fine + 1) * c, cols]
                ql, kl = qh * e_lv, kh * e_lv
            pair = _dot_nt(ql.astype(BF16), kl.astype(BF16))
            att += pair if lv == 0 else pair * m_ref[lv * c:(lv + 1) * c, :]
        e_inc = e_all[inc_rows, cols]
        st = st_ref[h]
        o_h = _dot(att.astype(BF16), v_bf[:, cols])
        o_h += _dot_nt((qh * e_inc).astype(BF16), st.astype(BF16))
        k_dec = (kh * jnp.exp2(inc[last_row:last_row + 1, :] - inc)).astype(BF16)
        st_ref[h] = st * e_inc[last_row:last_row + 1, :] + _dot(vt_bf[cols, :], k_dec)
        outs.append(o_h)
    return jnp.concatenate(outs, axis=-1)


HGRN_STEP_CHUNKS = 16


def _hgrn_kernel(qf_ref, qb_ref, if_ref, ib_ref, af_ref, ab_ref, lbf_ref, lbb_ref,
                 df_ref, db_ref, mf_ref, mb_ref, o_ref, stf_ref, stb_ref, *, levels, n_fine, chunk):
    n = pl.program_id(1)
    n_steps = pl.num_programs(1)
    blk = HGRN_STEP_CHUNKS * chunk

    @pl.when(n == 0)
    def _():
        stf_ref[...] = jnp.zeros_like(stf_ref)
        stb_ref[...] = jnp.zeros_like(stb_ref)

    def run(q_ref, i_ref, a_ref, lb_ref, d_ref, m_ref, st_ref, sub, fwd):
        rows = slice(sub * chunk, (sub + 1) * chunk)
        v = i_ref[rows, :]
        return _hgrn_direction(q_ref[rows, :], v.astype(BF16), v.T.astype(BF16), a_ref[rows, :],
                               lb_ref[...], d_ref, m_ref, st_ref, levels=levels, n_fine=n_fine, fwd=fwd)

    o_f, o_b = [], [None] * HGRN_STEP_CHUNKS
    for sub in range(HGRN_STEP_CHUNKS):
        rev = HGRN_STEP_CHUNKS - 1 - sub
        o_f.append(run(qf_ref, if_ref, af_ref, lbf_ref, df_ref, mf_ref, stf_ref, sub, True))
        o_b[rev] = run(qb_ref, ib_ref, ab_ref, lbb_ref, db_ref, mb_ref, stb_ref, rev, False)
    o_f = jnp.concatenate(o_f, axis=0)
    o_b = jnp.concatenate(o_b, axis=0)
    rows_f = pl.ds(pl.multiple_of(n * blk, blk), blk)
    rows_b = pl.ds(pl.multiple_of((n_steps - 1 - n) * blk, blk), blk)

    @pl.when(2 * n < n_steps)
    def _():
        o_ref[rows_f, :] = o_f
        o_ref[rows_b, :] = o_b

    @pl.when(2 * n >= n_steps)
    def _():
        o_ref[rows_f, :] += o_f
        o_ref[rows_b, :] += o_b


def hgrn2(q, i, a_fwd, a_bwd, lb, bsz, seq, chunk=HGRN_CHUNK):
    blk = HGRN_STEP_CHUNKS * chunk
    n_steps = seq // blk
    assert n_steps % 2 == 0
    d_np, m_np, levels, n_fine = _hgrn_tables(chunk)
    d_f = jnp.asarray(d_np, BF16)
    d_b = jnp.asarray(_flip_blocks(d_np, chunk), BF16)
    m_f = jnp.asarray(m_np, F32)
    m_b = jnp.asarray(_flip_blocks(m_np, chunk), F32)

    fwd = pl.BlockSpec((blk, MIX_W), lambda b, n: (b * n_steps + n, 0))
    bwd = pl.BlockSpec((blk, MIX_W), lambda b, n: (b * n_steps + n_steps - 1 - n, 0))
    vec = pl.BlockSpec((1, MIX_W), lambda b, n: (0, 0))

    def table(a):
        return pl.BlockSpec(a.shape, lambda b, n: (0, 0))

    return pl.pallas_call(
        functools.partial(_hgrn_kernel, levels=levels, n_fine=n_fine, chunk=chunk),
        grid=(bsz, n_steps),
        in_specs=[fwd, bwd, fwd, bwd, fwd, bwd, vec, vec,
                  table(d_f), table(d_b), table(m_f), table(m_b)],
        out_specs=pl.BlockSpec((seq, MIX_W), lambda b, n: (b, 0)),
        out_shape=jax.ShapeDtypeStruct((bsz * seq, MIX_W), F32),
        scratch_shapes=[pltpu.VMEM((N_HEADS, HEAD_W, HEAD_W), F32),
                        pltpu.VMEM((N_HEADS, HEAD_W, HEAD_W), F32)],
        compiler_params=_cparams(("parallel", "arbitrary")),
        name="hgrn2",
    )(q, q, i, i, a_fwd, a_bwd, lb[0:1], lb[1:2], d_f, d_b, m_f, m_b)


ATTN_TQ = 4096
ATTN_TW = 512


def _qk_prep(x, gain, cos, sin_signed, seg_mean):
    hi, lo = _split2(x * x)
    ms = _dot(hi, seg_mean) + _dot(lo, seg_mean)
    y = x * lax.rsqrt(ms + EPS) * gain
    lane = lax.broadcasted_iota(jnp.int32, y.shape, 1)
    half = QK_DIM // 2
    partner = jnp.where((lane % QK_DIM) < half,
                        pltpu.roll(y, HEAD_W - half, 1), pltpu.roll(y, half, 1))
    return y * cos + partner * sin_signed


ATTN_ONES_ROWS = 16
ATTN_KB = 2048


def _attn_kernel(q_ref, k_ref, v_ref, cosq_ref, sinq_ref, cosk_ref, sink_ref, qg_ref, kg_ref,
                 sg_ref, lam_ref, mean_ref, o_ref, kh_ref, v1t_ref, qp_ref, s0_ref, s1_ref, pa_ref, pb_ref,
                 acc_ref, *, seq):
    qi = pl.program_id(2)
    seg_mean = mean_ref[...]
    n_kb = seq // ATTN_KB

    @pl.when(qi == 0)
    def _():
        def prep(r, carry):
            rows = pl.ds(pl.multiple_of(r * ATTN_KB, ATTN_KB), ATTN_KB)
            kh_ref[r] = _qk_prep(k_ref[rows, :], kg_ref[...], cosk_ref[rows, :],
                                 sink_ref[rows, :], seg_mean).astype(BF16)
            v1t_ref[r, 0:HEAD_W, :] = v_ref[rows, :].T.astype(BF16)
            v1t_ref[r, HEAD_W:HEAD_W + ATTN_ONES_ROWS, :] = jnp.ones((ATTN_ONES_ROWS, ATTN_KB), BF16)
            return carry

        lax.fori_loop(0, n_kb, prep, 0)

    tw = ATTN_TW
    n_units = 2 * (q_ref.shape[0] // tw)
    scale = (QK_DIM ** -0.5) * LOG2E
    for t in range(n_units // 2):
        rows = slice(t * tw, (t + 1) * tw)
        qh = _qk_prep(q_ref[rows, :], qg_ref[...], cosq_ref[rows, :], sinq_ref[rows, :], seg_mean) * scale
        lane = lax.broadcasted_iota(jnp.int32, qh.shape, 1)
        qp_ref[2 * t] = jnp.where(lane < QK_DIM, qh, 0.0).astype(BF16)
        qp_ref[2 * t + 1] = jnp.where(lane >= QK_DIM, qh, 0.0).astype(BF16)

    s_refs, p_refs = (s0_ref, s1_ref), (pa_ref, pb_ref)
    neg = jnp.full((8, tw), -jnp.inf, F32)

    def phase(us, ue, m_e):
        def sc(kb, m8):
            if us is None:
                return m8
            s_t = _dot_nt(kh_ref[kb], qp_ref[us])
            s_refs[us % 2][kb] = s_t
            return jnp.maximum(m8, jnp.max(s_t.reshape(ATTN_KB // 8, 8, tw), axis=0))

        def pr(kb, slot):
            if ue is not None:
                p_refs[slot][...] = jnp.exp2(s_refs[ue % 2][kb] - m_e).astype(BF16)

        def va(kb, slot, first=False):
            if ue is not None:
                part = _dot(v1t_ref[kb], p_refs[slot][...])
                acc_ref[ue] = part if first else acc_ref[ue] + part

        m8 = neg
        for kb in range(n_kb):
            m8 = sc(kb, m8)
            pr(kb, kb % 2)
            if kb > 0:
                va(kb - 1, (kb - 1) % 2, first=(kb == 1))
        va(n_kb - 1, (n_kb - 1) % 2, first=(n_kb == 1))
        return jnp.max(m8, axis=0, keepdims=True)

    m = phase(0, None, None)
    for u in range(1, n_units):
        m = phase(u, u - 1, m)
    phase(None, n_units - 1, m)

    for t in range(n_units // 2):
        parts = [acc_ref[2 * t + c, 0:HEAD_W, :] / acc_ref[2 * t + c, HEAD_W:HEAD_W + 1, :]
                 for c in range(2)]
        o = (parts[0] - lam_ref[0, 0] * parts[1]).T
        o_ref[t * tw:(t + 1) * tw, :] = _rms_rows(o, sg_ref[...]) * lam_ref[0, 1]


def diff_attention(q, k, v, cos_t, sin_t, qg, kg, sub_g, lam, lambda_init, bsz, seq, tq=ATTN_TQ):
    nq = seq // tq
    n_kb = seq // ATTN_KB
    assert seq % ATTN_KB == 0
    seg = np.kron(np.eye(2, dtype=np.float32), np.full((QK_DIM, QK_DIM), 1.0 / QK_DIM, np.float32))
    qblk = pl.BlockSpec((tq, HEAD_W), lambda b, h, i: (b * nq + i, h))
    kblk = pl.BlockSpec((seq, HEAD_W), lambda b, h, i: (b, h))
    tabq = pl.BlockSpec((tq, HEAD_W), lambda b, h, i: (i, 0))
    tabk = pl.BlockSpec((seq, HEAD_W), lambda b, h, i: (0, 0))
    vec = pl.BlockSpec((1, HEAD_W), lambda b, h, i: (0, 0))
    return pl.pallas_call(
        functools.partial(_attn_kernel, seq=seq),
        grid=(bsz, N_HEADS, nq),
        in_specs=[qblk, kblk, kblk, tabq, tabq, tabk, tabk, vec, vec, vec,
                  pl.BlockSpec(memory_space=pltpu.SMEM),
                  pl.BlockSpec((HEAD_W, HEAD_W), lambda b, h, i: (0, 0))],
        out_specs=qblk,
        out_shape=jax.ShapeDtypeStruct((bsz * seq, MIX_W), F32),
        scratch_shapes=[pltpu.VMEM((n_kb, ATTN_KB, HEAD_W), BF16),
                        pltpu.VMEM((n_kb, HEAD_W + ATTN_ONES_ROWS, ATTN_KB), BF16),
                        pltpu.VMEM((2 * (tq // ATTN_TW), ATTN_TW, HEAD_W), BF16),
                        pltpu.VMEM((n_kb, ATTN_KB, ATTN_TW), F32),
                        pltpu.VMEM((n_kb, ATTN_KB, ATTN_TW), F32),
                        pltpu.VMEM((ATTN_KB, ATTN_TW), BF16),
                        pltpu.VMEM((ATTN_KB, ATTN_TW), BF16),
                        pltpu.VMEM((2 * (tq // ATTN_TW), HEAD_W + ATTN_ONES_ROWS, ATTN_TW), F32)],
        compiler_params=_cparams(("parallel", "parallel", "arbitrary")),
        name="diff_attention",
    )(q, k, v, cos_t, sin_t, cos_t, sin_t,
      jnp.tile(qg, 2).reshape(1, HEAD_W), jnp.tile(kg, 2).reshape(1, HEAD_W),
      sub_g.reshape(1, HEAD_W),
      jnp.stack([lam, jnp.asarray(1.0 - lambda_init, F32)]).reshape(1, 2).astype(F32),
      jnp.asarray(seg, BF16))


def _rope_tables(seq):
    inv = 1.0 / (ROPE_THETA ** (jnp.arange(0, QK_DIM, 2, dtype=F32) / QK_DIM))
    ang = jnp.arange(seq, dtype=F32)[:, None] * inv[None, :]
    cos, sin = jnp.cos(ang), jnp.sin(ang)
    return jnp.tile(cos, (1, 4)), jnp.tile(jnp.concatenate([-sin, sin], axis=-1), (1, 2))


def kernel(x, norm_mix_g, norm_mlp_g, w_in_even, w_out_even, sgu_norm_g, sgu_w, sgu_b,
           hgrn_lb_logits, hgrn_norm_g, w_in_odd, w_out_odd, conv_w, q_norm_g, k_norm_g,
           lambda_q1, lambda_k1, lambda_q2, lambda_k2, diff_norm_g, mlp_w1, mlp_w2):
    bsz, seq, d = x.shape
    depth = norm_mix_g.shape[0]
    cos_t, sin_t = _rope_tables(seq)
    p_lb = jax.nn.softmax(hgrn_lb_logits.astype(F32), axis=1)
    lower_bounds = jnp.cumsum(p_lb, axis=1) - p_lb[:, :1]

    w1_bf, w2_bf = mlp_w1.astype(BF16), mlp_w2.astype(BF16)
    wo_even_bf, wo_odd_bf = w_out_even.astype(BF16), w_out_odd.astype(BF16)

    x2 = x.reshape(bsz * seq, d)
    for l in range(depth):
        if l % 2 == 0:
            e = l // 2
            u, v, q, i, g, a_fwd, a_bwd = norm_proj(x2, norm_mix_g[l], w_in_even, e)
            b_full = jnp.broadcast_to(sgu_b[e][:, :, None], (N_HEADS, SGU_CHUNK, HEAD_W))
            o_sum = hgrn2(q, i, a_fwd, a_bwd, lower_bounds[:, e], bsz, seq)
            x2 = mix_mlp(x2, (u, v, o_sum, g), wo_even_bf, e, norm_mlp_g[l], w1_bf, w2_bf, l, seq,
                         sgu_params=(sgu_norm_g[e], sgu_w[e].astype(BF16), b_full, hgrn_norm_g[e]))
        else:
            o = l // 2
            h_in, b_gate, c_gate, q, k, v = norm_proj(x2, norm_mix_g[l], w_in_odd, o)
            lambda_init = 0.8 - 0.6 * math.exp(-0.3 * l)
            lam = (jnp.exp(jnp.sum(lambda_q1[o] * lambda_k1[o]))
                   - jnp.exp(jnp.sum(lambda_q2[o] * lambda_k2[o])) + lambda_init)
            out_d = diff_attention(q, k, v, cos_t, sin_t, q_norm_g[o], k_norm_g[o], diff_norm_g[o],
                                   lam, lambda_init, bsz, seq)
            x2 = mix_mlp(x2, (h_in, b_gate, c_gate, out_d), wo_odd_bf, o, norm_mlp_g[l], w1_bf, w2_bf, l,
                         seq, conv_w=conv_w[o])
    return x2.reshape(bsz, seq, d)
```

```python
import functools
import math

import numpy as np
import jax
import jax.numpy as jnp
from jax import lax
from jax.experimental import pallas as pl
from jax.experimental.pallas import tpu as pltpu

F32 = jnp.float32
BF16 = jnp.bfloat16

EPS = 1e-6
ROPE_THETA = 10000.0
LOG2E = 1.4426950408889634

HEAD_W = 128
N_HEADS = 4
MIX_W = HEAD_W * N_HEADS
SGU_CHUNK = 128
HGRN_CHUNK = 128
QK_DIM = 64

VMEM_LIMIT = 56 * 1024 * 1024


def _cparams(sem):
    return pltpu.CompilerParams(dimension_semantics=sem, vmem_limit_bytes=VMEM_LIMIT)


def _dot(a, b):
    return jnp.dot(a, b, preferred_element_type=F32)


def _dot_nt(a, b):
    return lax.dot_general(a, b, (((1,), (1,)), ((), ())), preferred_element_type=F32)


def _split2(x):
    hi = x.astype(BF16)
    lo = (x - hi.astype(F32)).astype(BF16)
    return hi, lo


def _rms_rows(x, g):
    ms = jnp.mean(x * x, axis=-1, keepdims=True)
    return x * lax.rsqrt(ms + EPS) * g


def _norm_proj_kernel(x_ref, g_ref, w_ref, *rest):
    o_refs, wbf_ref = rest[:-1], rest[-1]

    @pl.when(pl.program_id(0) == 0)
    def _():
        for s in range(len(o_refs)):
            cols = slice(s * MIX_W, (s + 1) * MIX_W)
            wbf_ref[:, cols] = w_ref[:, cols].astype(BF16)

    h = _rms_rows(x_ref[...], g_ref[...]).astype(BF16)
    for s, o_ref in enumerate(o_refs):
        o_ref[...] = _dot(h, wbf_ref[:, s * MIX_W:(s + 1) * MIX_W])


def norm_proj(x2, g, w_stack, layer, tm=512):
    t, d = x2.shape
    n = w_stack.shape[2]
    n_split = n // MIX_W
    return pl.pallas_call(
        _norm_proj_kernel,
        grid=(t // tm,),
        in_specs=[
            pl.BlockSpec((tm, d), lambda i: (i, 0)),
            pl.BlockSpec((1, d), lambda i: (0, 0)),
            pl.BlockSpec((None, d, n), lambda i: (layer, 0, 0), pipeline_mode=pl.Buffered(1)),
        ],
        out_specs=[pl.BlockSpec((tm, MIX_W), lambda i: (i, 0)) for _ in range(n_split)],
        out_shape=[jax.ShapeDtypeStruct((t, MIX_W), F32) for _ in range(n_split)],
        scratch_shapes=[pltpu.VMEM((d, n), BF16)],
        compiler_params=_cparams(("arbitrary",)),
        name="norm_proj",
    )(x2, g.reshape(1, d), w_stack)


def _head_norm_gate(o, g, ng):
    outs = []
    for h in range(N_HEADS):
        cols = slice(h * HEAD_W, (h + 1) * HEAD_W)
        gh = g[:, cols]
        outs.append(_rms_rows(o[:, cols], ng[:, cols]) * (gh * (1.0 / (1.0 + jnp.exp(-gh)))))
    return jnp.concatenate(outs, axis=-1)


def _sgu_block(u, v, g, w_ref, b_ref):
    outs = []
    for h in range(N_HEADS):
        cols = slice(h * HEAD_W, (h + 1) * HEAD_W)
        vn = _rms_rows(jax.nn.gelu(v[:, cols]), g[:, cols])
        mixed = _dot(w_ref[h], vn.astype(BF16)) + b_ref[h]
        outs.append(jax.nn.gelu(u[:, cols]) * mixed)
    return jnp.concatenate(outs, axis=-1)


CONV_HALO = 8


def _mix_mlp_kernel(*refs, even, n_tiles, seq):
    if even:
        (x_ref, u_ref, v_ref, b_ref, gate_ref, sg_ref, ws_ref, bs_ref, ng_ref,
         wo_ref, g_ref, w1_ref, w2_ref, o_ref, h0_ref, h1_ref, acc0_ref, acc1_ref) = refs
    else:
        (x_ref, hin_ref, bg_ref, cg_ref, b_ref, hprev_ref, cprev_ref, hnext_ref, cnext_ref, cw_ref,
         wo_ref, g_ref, w1_ref, w2_ref, o_ref, h0_ref, h1_ref, acc0_ref, acc1_ref) = refs
    h_refs, acc_refs = (h0_ref, h1_ref), (acc0_ref, acc1_ref)
    i, j = pl.program_id(0), pl.program_id(1)
    rb = x_ref.shape[0]
    rows = pl.ds(pl.multiple_of(j * rb, rb), rb)

    def conv_block():
        pos = ((jnp.minimum(i, n_tiles - 1) * pl.num_programs(1) + j) * rb) % seq
        z = cg_ref[...] * hin_ref[...]
        z_before = (cprev_ref[...] * hprev_ref[...])[CONV_HALO - 1:CONV_HALO, :]
        z_after = (cnext_ref[...] * hnext_ref[...])[0:1, :]
        z_before = jnp.where(pos == 0, 0.0, z_before)
        z_after = jnp.where(pos + rb == seq, 0.0, z_after)
        r = lax.broadcasted_iota(jnp.int32, z.shape, 0)
        prev = jnp.where(r == 0, z_before, pltpu.roll(z, 1, 0))
        nxt = jnp.where(r == rb - 1, z_after, pltpu.roll(z, rb - 1, 0))
        y = cw_ref[0:1, :] * prev + cw_ref[1:2, :] * z + cw_ref[2:3, :] * nxt
        return bg_ref[...] * y

    def prepare(slot):
        if even:
            a = jnp.concatenate(
                [_sgu_block(u_ref[r:r + SGU_CHUNK, :], v_ref[r:r + SGU_CHUNK, :], sg_ref[...], ws_ref, bs_ref)
                 for r in range(0, rb, SGU_CHUNK)], axis=0)
            b = _head_norm_gate(b_ref[...], gate_ref[...], ng_ref[...])
        else:
            a, b = conv_block(), b_ref[...]
        x1 = x_ref[...] + _dot(a.astype(BF16), wo_ref[:MIX_W, :])
        x1 += _dot(b.astype(BF16), wo_ref[MIX_W:, :])
        acc_refs[slot][rows, :] = x1
        h_refs[slot][rows, :] = _rms_rows(x1, g_ref[...]).astype(BF16)

    def mlp_chunk(slot):
        hm = _dot(h_refs[slot][...], w1_ref[...])
        act = jnp.square(jnp.maximum(hm, 0.0)).astype(BF16)
        acc_refs[slot][...] += _dot(act, w2_ref[...])

    odd_step = (i % 2) == 1

    @pl.when(i == 0)
    def _():
        prepare(0)

    @pl.when((i > 0) & (i < n_tiles) & jnp.logical_not(odd_step))
    def _():
        mlp_chunk(1)
        prepare(0)

    @pl.when(odd_step)
    def _():
        mlp_chunk(0)
        prepare(1)

    @pl.when(i == n_tiles)
    def _():
        mlp_chunk(1)

    last = j == pl.num_programs(1) - 1

    @pl.when(last & odd_step)
    def _():
        o_ref[...] = acc0_ref[...]

    @pl.when(last & (i > 0) & jnp.logical_not(odd_step))
    def _():
        o_ref[...] = acc1_ref[...]


def mix_mlp(x2, mix_in, wo_bf, wo_layer, g, w1_bf, w2_bf, layer, seq, sgu_params=None, conv_w=None,
            tm=1024, tf=1024):
    t, d = x2.shape
    ff = w1_bf.shape[2]
    n_tiles, n_j = t // tm, ff // tf
    rb = tm // n_j
    assert n_tiles % 2 == 0 and rb % SGU_CHUNK == 0 and seq % rb == 0
    even = sgu_params is not None

    def row_block(i, j):
        return jnp.minimum(i, n_tiles - 1) * n_j + j

    def blk(width):
        return pl.BlockSpec((rb, width), lambda i, j: (row_block(i, j), 0))

    def const(shape):
        return pl.BlockSpec(shape, lambda i, j: (0,) * len(shape))

    ins = [x2] + list(mix_in)
    specs = [blk(d)] + [blk(MIX_W) for _ in mix_in]
    if even:
        norm_g, w_s_bf, b_full, hgrn_norm_g = sgu_params
        ins += [norm_g.reshape(1, MIX_W), w_s_bf, b_full, hgrn_norm_g.reshape(1, MIX_W)]
        specs += [const((1, MIX_W)), const((N_HEADS, SGU_CHUNK, SGU_CHUNK)),
                  const((N_HEADS, SGU_CHUNK, HEAD_W)), const((1, MIX_W))]
    else:
        h_in, _, c_gate, _ = mix_in
        per = rb // CONV_HALO
        before = pl.BlockSpec((CONV_HALO, MIX_W),
                              lambda i, j: (jnp.maximum(row_block(i, j) * per - 1, 0), 0))
        after = pl.BlockSpec((CONV_HALO, MIX_W),
                             lambda i, j: (jnp.minimum((row_block(i, j) + 1) * per, t // CONV_HALO - 1), 0))
        ins += [h_in, c_gate, h_in, c_gate, conv_w]
        specs += [before, before, after, after, const((3, MIX_W))]
    return pl.pallas_call(
        functools.partial(_mix_mlp_kernel, even=even, n_tiles=n_tiles, seq=seq),
        grid=(n_tiles + 1, n_j),
        in_specs=specs + [
            pl.BlockSpec((None, 2 * MIX_W, d), lambda i, j: (wo_layer, 0, 0)),
            const((1, d)),
            pl.BlockSpec((None, d, tf), lambda i, j: (layer, 0, j)),
            pl.BlockSpec((None, tf, d), lambda i, j: (layer, j, 0)),
        ],
        out_specs=pl.BlockSpec((tm, d), lambda i, j: (jnp.maximum(i - 1, 0), 0)),
        out_shape=jax.ShapeDtypeStruct((t, d), F32),
        scratch_shapes=[pltpu.VMEM((tm, d), BF16), pltpu.VMEM((tm, d), BF16),
                        pltpu.VMEM((tm, d), F32), pltpu.VMEM((tm, d), F32)],
        compiler_params=_cparams(("arbitrary", "arbitrary")),
        name="mix_mlp",
    )(*ins, wo_bf, g.reshape(1, d), w1_bf, w2_bf)


SUBLANES = 8


def _hgrn_tables(c):
    levels = int(math.log2(c))
    assert 1 << levels == c
    pos = np.arange(c)
    t = pos[:, None]
    i = pos[None, :]
    d_rows, m_rows = [], []
    for lv in range(levels):
        h = c >> (lv + 1)
        start = (t // (2 * h)) * (2 * h)
        mid = start + h - 1
        upper = (t - start) >= h
        if h < SUBLANES:
            d_rows.append(np.where(upper, (i > mid) & (i <= t), (i > t) & (i <= mid)))
        same = (t // (2 * h)) == (i // (2 * h))
        m_rows.append(same & upper & ((i % (2 * h)) < h))
    n_fine = len(d_rows)
    d_rows.append(i <= t)
    d = np.concatenate(d_rows, axis=0).astype(np.float32)
    m = np.concatenate(m_rows, axis=0).astype(np.float32)
    return d, m, levels, n_fine


def _flip_blocks(a, c):
    blocks = a.reshape(-1, c, c)
    return blocks[:, ::-1, ::-1].reshape(a.shape)


def _coarse_level(qh, kh, inc, h, fwd):
    c = inc.shape[0]
    zeros = jnp.zeros((h, inc.shape[1]), F32)
    q_rows, k_rows = [], []
    for n in range(c // (2 * h)):
        lo = slice(n * 2 * h, n * 2 * h + h)
        up = slice(n * 2 * h + h, (n + 1) * 2 * h)
        if fwd:
            ref = inc[n * 2 * h + h - 1:n * 2 * h + h, :]
            q_rows += [zeros, qh[up] * jnp.exp2(inc[up] - ref)]
            k_rows += [kh[lo] * jnp.exp2(ref - inc[lo]), zeros]
        else:
            ref = inc[n * 2 * h + h:n * 2 * h + h + 1, :]
            q_rows += [qh[lo] * jnp.exp2(inc[lo] - ref), zeros]
            k_rows += [zeros, kh[up] * jnp.exp2(ref - inc[up])]
    return jnp.concatenate(q_rows, axis=0), jnp.concatenate(k_rows, axis=0)


def _hgrn_direction(q, v_bf, vt_bf, a, lb, d_ref, m_ref, st_ref, *, levels, n_fine, fwd):
    c = q.shape[0]
    f = lb + (1.0 - lb) * (1.0 / (1.0 + jnp.exp(-a)))
    lf = jnp.log(f) * LOG2E
    kk = 1.0 - f
    hi, lo = _split2(lf)
    sums = _dot(d_ref[...], hi) + _dot(d_ref[...], lo)
    e_all = jnp.exp2(sums)
    inc_rows = slice(n_fine * c, (n_fine + 1) * c)
    last_row = c - 1 if fwd else 0
    row = lax.broadcasted_iota(jnp.int32, (c, c), 0)
    col = lax.broadcasted_iota(jnp.int32, (c, c), 1)
    eye = row == col
    outs = []
    for h in range(N_HEADS):
        cols = slice(h * HEAD_W, (h + 1) * HEAD_W)
        qh, kh, inc = q[:, cols], kk[:, cols], sums[inc_rows, cols]
        att = jnp.where(eye, jnp.sum(qh * kh, axis=-1, keepdims=True), 0.0)
        for lv in range(levels):
            half = c >> (lv + 1)
            if half >= SUBLANES:
                ql, kl = _coarse_level(qh, kh, inc, half, fwd)
            else:
                fine = lv - (levels - n_fine)
                e_lv = e_all[fine * c:(fine + 1) * c, cols]
                ql, kl = qh * e_lv, kh * e_lv
            pair = _dot_nt(ql.astype(BF16), kl.astype(BF16))
            att += pair if lv == 0 else pair * m_ref[lv * c:(lv + 1) * c, :]
        e_inc = e_all[inc_rows, cols]
        st = st_ref[h]
        o_h = _dot(att.astype(BF16), v_bf[:, cols])
        o_h += _dot_nt((qh * e_inc).astype(BF16), st.astype(BF16))
        k_dec = (kh * jnp.exp2(inc[last_row:last_row + 1, :] - inc)).astype(BF16)
        st_ref[h] = st * e_inc[last_row:last_row + 1, :] + _dot(vt_bf[cols, :], k_dec)
        outs.append(o_h)
    return jnp.concatenate(outs, axis=-1)


HGRN_STEP_CHUNKS = 8


def _hgrn_kernel(qf_ref, qb_ref, if_ref, ib_ref, af_ref, ab_ref, lbf_ref, lbb_ref,
                 df_ref, db_ref, mf_ref, mb_ref, o_ref, stf_ref, stb_ref, *, levels, n_fine, chunk):
    n = pl.program_id(1)
    n_steps = pl.num_programs(1)
    blk = HGRN_STEP_CHUNKS * chunk

    @pl.when(n == 0)
    def _():
        stf_ref[...] = jnp.zeros_like(stf_ref)
        stb_ref[...] = jnp.zeros_like(stb_ref)

    def run(q_ref, i_ref, a_ref, lb_ref, d_ref, m_ref, st_ref, sub, fwd):
        rows = slice(sub * chunk, (sub + 1) * chunk)
        v = i_ref[rows, :]
        return _hgrn_direction(q_ref[rows, :], v.astype(BF16), v.T.astype(BF16), a_ref[rows, :],
                               lb_ref[...], d_ref, m_ref, st_ref, levels=levels, n_fine=n_fine, fwd=fwd)

    o_f, o_b = [], [None] * HGRN_STEP_CHUNKS
    for sub in range(HGRN_STEP_CHUNKS):
        rev = HGRN_STEP_CHUNKS - 1 - sub
        o_f.append(run(qf_ref, if_ref, af_ref, lbf_ref, df_ref, mf_ref, stf_ref, sub, True))
        o_b[rev] = run(qb_ref, ib_ref, ab_ref, lbb_ref, db_ref, mb_ref, stb_ref, rev, False)
    o_f = jnp.concatenate(o_f, axis=0)
    o_b = jnp.concatenate(o_b, axis=0)
    rows_f = pl.ds(pl.multiple_of(n * blk, blk), blk)
    rows_b = pl.ds(pl.multiple_of((n_steps - 1 - n) * blk, blk), blk)

    @pl.when(2 * n < n_steps)
    def _():
        o_ref[rows_f, :] = o_f
        o_ref[rows_b, :] = o_b

    @pl.when(2 * n >= n_steps)
    def _():
        o_ref[rows_f, :] += o_f
        o_ref[rows_b, :] += o_b


def hgrn2(q, i, a_fwd, a_bwd, lb, bsz, seq, chunk=HGRN_CHUNK):
    blk = HGRN_STEP_CHUNKS * chunk
    n_steps = seq // blk
    assert n_steps % 2 == 0
    d_np, m_np, levels, n_fine = _hgrn_tables(chunk)
    d_f = jnp.asarray(d_np, BF16)
    d_b = jnp.asarray(_flip_blocks(d_np, chunk), BF16)
    m_f = jnp.asarray(m_np, F32)
    m_b = jnp.asarray(_flip_blocks(m_np, chunk), F32)

    fwd = pl.BlockSpec((blk, MIX_W), lambda b, n: (b * n_steps + n, 0))
    bwd = pl.BlockSpec((blk, MIX_W), lambda b, n: (b * n_steps + n_steps - 1 - n, 0))
    vec = pl.BlockSpec((1, MIX_W), lambda b, n: (0, 0))

    def table(a):
        return pl.BlockSpec(a.shape, lambda b, n: (0, 0))

    return pl.pallas_call(
        functools.partial(_hgrn_kernel, levels=levels, n_fine=n_fine, chunk=chunk),
        grid=(bsz, n_steps),
        in_specs=[fwd, bwd, fwd, bwd, fwd, bwd, vec, vec,
                  table(d_f), table(d_b), table(m_f), table(m_b)],
        out_specs=pl.BlockSpec((seq, MIX_W), lambda b, n: (b, 0)),
        out_shape=jax.ShapeDtypeStruct((bsz * seq, MIX_W), F32),
        scratch_shapes=[pltpu.VMEM((N_HEADS, HEAD_W, HEAD_W), F32),
                        pltpu.VMEM((N_HEADS, HEAD_W, HEAD_W), F32)],
        compiler_params=_cparams(("parallel", "arbitrary")),
        name="hgrn2",
    )(q, q, i, i, a_fwd, a_bwd, lb[0:1], lb[1:2], d_f, d_b, m_f, m_b)


ATTN_TQ = 4096
ATTN_TW = 512


def _qk_prep(x, gain, cos, sin_signed, seg_mean):
    hi, lo = _split2(x * x)
    ms = _dot(hi, seg_mean) + _dot(lo, seg_mean)
    y = x * lax.rsqrt(ms + EPS) * gain
    lane = lax.broadcasted_iota(jnp.int32, y.shape, 1)
    half = QK_DIM // 2
    partner = jnp.where((lane % QK_DIM) < half,
                        pltpu.roll(y, HEAD_W - half, 1), pltpu.roll(y, half, 1))
    return y * cos + partner * sin_signed


ATTN_ONES_ROWS = 16
ATTN_KB = 4096


def _attn_kernel(q_ref, k_ref, v_ref, cosq_ref, sinq_ref, cosk_ref, sink_ref, qg_ref, kg_ref,
                 sg_ref, lam_ref, mean_ref, o_ref, kh_ref, v1t_ref, qp_ref, s0_ref, s1_ref, pa_ref, pb_ref,
                 acc_ref, *, seq):
    qi = pl.program_id(2)
    seg_mean = mean_ref[...]
    n_kb = seq // ATTN_KB

    @pl.when(qi == 0)
    def _():
        def prep(r, carry):
            rows = pl.ds(pl.multiple_of(r * ATTN_KB, ATTN_KB), ATTN_KB)
            kh_ref[r] = _qk_prep(k_ref[rows, :], kg_ref[...], cosk_ref[rows, :],
                                 sink_ref[rows, :], seg_mean).astype(BF16)
            v1t_ref[r, 0:HEAD_W, :] = v_ref[rows, :].T.astype(BF16)
            v1t_ref[r, HEAD_W:HEAD_W + ATTN_ONES_ROWS, :] = jnp.ones((ATTN_ONES_ROWS, ATTN_KB), BF16)
            return carry

        lax.fori_loop(0, n_kb, prep, 0)

    tw = ATTN_TW
    n_units = 2 * (q_ref.shape[0] // tw)
    scale = (QK_DIM ** -0.5) * LOG2E
    for t in range(n_units // 2):
        rows = slice(t * tw, (t + 1) * tw)
        qh = _qk_prep(q_ref[rows, :], qg_ref[...], cosq_ref[rows, :], sinq_ref[rows, :], seg_mean) * scale
        lane = lax.broadcasted_iota(jnp.int32, qh.shape, 1)
        qp_ref[2 * t] = jnp.where(lane < QK_DIM, qh, 0.0).astype(BF16)
        qp_ref[2 * t + 1] = jnp.where(lane >= QK_DIM, qh, 0.0).astype(BF16)

    s_refs, p_refs = (s0_ref, s1_ref), (pa_ref, pb_ref)
    neg = jnp.full((8, tw), -jnp.inf, F32)

    def phase(us, ue, m_e):
        def sc(kb, m8):
            if us is None:
                return m8
            s_t = _dot_nt(kh_ref[kb], qp_ref[us])
            s_refs[us % 2][kb] = s_t
            return jnp.maximum(m8, jnp.max(s_t.reshape(ATTN_KB // 8, 8, tw), axis=0))

        def pr(kb, slot):
            if ue is not None:
                p_refs[slot][...] = jnp.exp2(s_refs[ue % 2][kb] - m_e).astype(BF16)

        def va(kb, slot, first=False):
            if ue is not None:
                part = _dot(v1t_ref[kb], p_refs[slot][...])
                acc_ref[ue] = part if first else acc_ref[ue] + part

        m8 = neg
        for kb in range(n_kb):
            m8 = sc(kb, m8)
            pr(kb, kb % 2)
            if kb > 0:
                va(kb - 1, (kb - 1) % 2, first=(kb == 1))
        va(n_kb - 1, (n_kb - 1) % 2, first=(n_kb == 1))
        return jnp.max(m8, axis=0, keepdims=True)

    m = phase(0, None, None)
    for u in range(1, n_units):
        m = phase(u, u - 1, m)
    phase(None, n_units - 1, m)

    for t in range(n_units // 2):
        parts = [acc_ref[2 * t + c, 0:HEAD_W, :] / acc_ref[2 * t + c, HEAD_W:HEAD_W + 1, :]
                 for c in range(2)]
        o = (parts[0] - lam_ref[0, 0] * parts[1]).T
        o_ref[t * tw:(t + 1) * tw, :] = _rms_rows(o, sg_ref[...]) * lam_ref[0, 1]


def diff_attention(q, k, v, cos_t, sin_t, qg, kg, sub_g, lam, lambda_init, bsz, seq, tq=ATTN_TQ):
    nq = seq // tq
    n_kb = seq // ATTN_KB
    assert seq % ATTN_KB == 0
    seg = np.kron(np.eye(2, dtype=np.float32), np.full((QK_DIM, QK_DIM), 1.0 / QK_DIM, np.float32))
    qblk = pl.BlockSpec((tq, HEAD_W), lambda b, h, i: (b * nq + i, h))
    kblk = pl.BlockSpec((seq, HEAD_W), lambda b, h, i: (b, h))
    tabq = pl.BlockSpec((tq, HEAD_W), lambda b, h, i: (i, 0))
    tabk = pl.BlockSpec((seq, HEAD_W), lambda b, h, i: (0, 0))
    vec = pl.BlockSpec((1, HEAD_W), lambda b, h, i: (0, 0))
    return pl.pallas_call(
        functools.partial(_attn_kernel, seq=seq),
        grid=(bsz, N_HEADS, nq),
        in_specs=[qblk, kblk, kblk, tabq, tabq, tabk, tabk, vec, vec, vec,
                  pl.BlockSpec(memory_space=pltpu.SMEM),
                  pl.BlockSpec((HEAD_W, HEAD_W), lambda b, h, i: (0, 0))],
        out_specs=qblk,
        out_shape=jax.ShapeDtypeStruct((bsz * seq, MIX_W), F32),
        scratch_shapes=[pltpu.VMEM((n_kb, ATTN_KB, HEAD_W), BF16),
                        pltpu.VMEM((n_kb, HEAD_W + ATTN_ONES_ROWS, ATTN_KB), BF16),
                        pltpu.VMEM((2 * (tq // ATTN_TW), ATTN_TW, HEAD_W), BF16),
                        pltpu.VMEM((n_kb, ATTN_KB, ATTN_TW), F32),
                        pltpu.VMEM((n_kb, ATTN_KB, ATTN_TW), F32),
                        pltpu.VMEM((ATTN_KB, ATTN_TW), BF16),
                        pltpu.VMEM((ATTN_KB, ATTN_TW), BF16),
                        pltpu.VMEM((2 * (tq // ATTN_TW), HEAD_W + ATTN_ONES_ROWS, ATTN_TW), F32)],
        compiler_params=_cparams(("parallel", "parallel", "arbitrary")),
        name="diff_attention",
    )(q, k, v, cos_t, sin_t, cos_t, sin_t,
      jnp.tile(qg, 2).reshape(1, HEAD_W), jnp.tile(kg, 2).reshape(1, HEAD_W),
      sub_g.reshape(1, HEAD_W),
      jnp.stack([lam, jnp.asarray(1.0 - lambda_init, F32)]).reshape(1, 2).astype(F32),
      jnp.asarray(seg, BF16))


def _rope_tables(seq):
    inv = 1.0 / (ROPE_THETA ** (jnp.arange(0, QK_DIM, 2, dtype=F32) / QK_DIM))
    ang = jnp.arange(seq, dtype=F32)[:, None] * inv[None, :]
    cos, sin = jnp.cos(ang), jnp.sin(ang)
    return jnp.tile(cos, (1, 4)), jnp.tile(jnp.concatenate([-sin, sin], axis=-1), (1, 2))


def kernel(x, norm_mix_g, norm_mlp_g, w_in_even, w_out_even, sgu_norm_g, sgu_w, sgu_b,
           hgrn_lb_logits, hgrn_norm_g, w_in_odd, w_out_odd, conv_w, q_norm_g, k_norm_g,
           lambda_q1, lambda_k1, lambda_q2, lambda_k2, diff_norm_g, mlp_w1, mlp_w2):
    bsz, seq, d = x.shape
    depth = norm_mix_g.shape[0]
    cos_t, sin_t = _rope_tables(seq)
    p_lb = jax.nn.softmax(hgrn_lb_logits.astype(F32), axis=1)
    lower_bounds = jnp.cumsum(p_lb, axis=1) - p_lb[:, :1]

    w1_bf, w2_bf = mlp_w1.astype(BF16), mlp_w2.astype(BF16)
    wo_even_bf, wo_odd_bf = w_out_even.astype(BF16), w_out_odd.astype(BF16)

    x2 = x.reshape(bsz * seq, d)
    for l in range(depth):
        if l % 2 == 0:
            e = l // 2
            u, v, q, i, g, a_fwd, a_bwd = norm_proj(x2, norm_mix_g[l], w_in_even, e)
            b_full = jnp.broadcast_to(sgu_b[e][:, :, None], (N_HEADS, SGU_CHUNK, HEAD_W))
            o_sum = hgrn2(q, i, a_fwd, a_bwd, lower_bounds[:, e], bsz, seq)
            x2 = mix_mlp(x2, (u, v, o_sum, g), wo_even_bf, e, norm_mlp_g[l], w1_bf, w2_bf, l, seq,
                         sgu_params=(sgu_norm_g[e], sgu_w[e].astype(BF16), b_full, hgrn_norm_g[e]))
        else:
            o = l // 2
            h_in, b_gate, c_gate, q, k, v = norm_proj(x2, norm_mix_g[l], w_in_odd, o)
            lambda_init = 0.8 - 0.6 * math.exp(-0.3 * l)
            lam = (jnp.exp(jnp.sum(lambda_q1[o] * lambda_k1[o]))
                   - jnp.exp(jnp.sum(lambda_q2[o] * lambda_k2[o])) + lambda_init)
            out_d = diff_attention(q, k, v, cos_t, sin_t, q_norm_g[o], k_norm_g[o], diff_norm_g[o],
                                   lam, lambda_init, bsz, seq)
            x2 = mix_mlp(x2, (h_in, b_gate, c_gate, out_d), wo_odd_bf, o, norm_mlp_g[l], w1_bf, w2_bf, l,
                         seq, conv_w=conv_w[o])
    return x2.reshape(bsz, seq, d)
```
